```python
import math
import jax, jax.numpy as jnp
from jax import lax
import numpy as np

D_MODEL = 1024
BATCH = 1
SEQ = 16384
DEPTH = 2
DEC_BATCH = 8
DEC_SEQ = 2048
PAST_LEN = 128

N_MIXERS = 2
N_ATTN_LAYERS = (DEPTH + 1) // 2
N_FNET_LAYERS = DEPTH // 2
DA_HEADS = 8
DA_HEAD_DIM = D_MODEL // DA_HEADS // 2
DA_V_DIM = 2 * DA_HEAD_DIM
FN_GROUPS = 8
FN_GROUP_DIM = D_MODEL // FN_GROUPS
Q_BLOCK = 128
EPS = 1e-6
SUBLN_EPS = 1e-5
LAMBDA_STD = 0.1

kernel_name = "diffattn_fnet_interleaved_encoder"


def lambda_init_fn(layer_idx):
    return 0.8 - 0.6 * math.exp(-0.3 * layer_idx)


def rmsnorm(x, g, eps=EPS):
    xf = x.astype(jnp.float32)
    xf = xf * lax.rsqrt(jnp.mean(xf * xf, axis=-1, keepdims=True) + eps)
    return (xf * g.astype(jnp.float32)).astype(x.dtype)


def alibi_slopes(n_heads):
    return 2.0 ** (-(8.0 / n_heads) * jnp.arange(1, n_heads + 1, dtype=jnp.float32))


def diff_attn_branch(h, w_in, w_out, lq1, lk1, lq2, lk2, subln_g, lam_init):
    B, S, D = h.shape
    H, d = DA_HEADS, DA_HEAD_DIM
    proj = h @ w_in
    q, k, v, z = jnp.split(proj, 4, axis=-1)
    q = q.reshape(B, S, H, 2, d).transpose(0, 2, 3, 1, 4)
    k = k.reshape(B, S, H, 2, d).transpose(0, 2, 3, 1, 4)
    v = v.reshape(B, S, H, DA_V_DIM).transpose(0, 2, 1, 3)
    f32 = jnp.float32
    lam = (jnp.exp(jnp.sum(lq1.astype(f32) * lk1.astype(f32)))
           - jnp.exp(jnp.sum(lq2.astype(f32) * lk2.astype(f32))) + lam_init)
    slopes = alibi_slopes(H)
    kpos = jnp.arange(S, dtype=f32)
    scale = d ** -0.5

    def block(start):
        qb = lax.dynamic_slice_in_dim(q, start, Q_BLOCK, axis=3)
        s = jnp.einsum('bhmqd,bhmkd->bhmqk', qb, k,
                       preferred_element_type=f32) * scale
        qpos = start.astype(f32) + jnp.arange(Q_BLOCK, dtype=f32)
        dist = jnp.abs(qpos[:, None] - kpos[None, :])
        s = s - (slopes[:, None, None] * dist)[None, :, None]
        p = jax.nn.softmax(s, axis=-1)
        pd = p[:, :, 0] - lam * p[:, :, 1]
        return jnp.einsum('bhqk,bhkv->bhqv', pd.astype(v.dtype), v)

    starts = jnp.arange(S // Q_BLOCK, dtype=jnp.int32) * Q_BLOCK
    o = lax.map(block, starts)
    o = o.transpose(1, 0, 3, 2, 4).reshape(B, S, H, DA_V_DIM)
    o = rmsnorm(o, subln_g, SUBLN_EPS) * (1.0 - lam_init)
    o = o.reshape(B, S, D) * jax.nn.silu(z)
    return o @ w_out


def fourier_branch(h, w_in, w_out):
    B, S, D = h.shape
    u, z = jnp.split(h @ w_in, 2, axis=-1)
    ug = u.astype(jnp.float32).reshape(B, S, FN_GROUPS, FN_GROUP_DIM)
    f = jnp.fft.fft2(ug, axes=(1, 3), norm="ortho").real
    f = f.reshape(B, S, D).astype(h.dtype)
    return (f * jax.nn.silu(z)) @ w_out


def trunk(x, attn_norm, attn_w_in, attn_lambda_q1, attn_lambda_k1, attn_lambda_q2,
          attn_lambda_k2, attn_subln, attn_w_out, fnet_norm, fnet_w_in, fnet_w_out, final_norm):
    for i in range(DEPTH):
        j = i // N_MIXERS
        if i % N_MIXERS == 0:
            h = rmsnorm(x, attn_norm[j])
            x = x + diff_attn_branch(h, attn_w_in[j], attn_w_out[j], attn_lambda_q1[j],
                                     attn_lambda_k1[j], attn_lambda_q2[j], attn_lambda_k2[j],
                                     attn_subln[j], lambda_init_fn(i))
        else:
            h = rmsnorm(x, fnet_norm[j])
            x = x + fourier_branch(h, fnet_w_in[j], fnet_w_out[j])
    return rmsnorm(x, final_norm)


def setup_inputs(seed: int = 0) -> dict:
    key = jax.random.key(seed)
    ks = jax.random.split(key, 16)
    D = D_MODEL
    na, nf = N_ATTN_LAYERS, N_FNET_LAYERS
    nrm = jax.random.normal
    return {
        "x_prompt": nrm(ks[0], (BATCH, SEQ, D), jnp.float32),
        "x_sample": nrm(ks[1], (DEC_BATCH, DEC_SEQ, D), jnp.float32),
        "attn_norm": 1.0 + 0.01 * nrm(ks[2], (na, D), jnp.float32),
        "attn_w_in": nrm(ks[3], (na, D, 4 * D), jnp.float32) * D ** -0.5,
        "attn_lambda_q1": LAMBDA_STD * nrm(ks[4], (na, DA_HEAD_DIM), jnp.float32),
        "attn_lambda_k1": LAMBDA_STD * nrm(ks[5], (na, DA_HEAD_DIM), jnp.float32),
        "attn_lambda_q2": LAMBDA_STD * nrm(ks[6], (na, DA_HEAD_DIM), jnp.float32),
        "attn_lambda_k2": LAMBDA_STD * nrm(ks[7], (na, DA_HEAD_DIM), jnp.float32),
        "attn_subln": 1.0 + 0.01 * nrm(ks[8], (na, DA_V_DIM), jnp.float32),
        "attn_w_out": nrm(ks[9], (na, D, D), jnp.float32) * D ** -0.5,
        "fnet_norm": 1.0 + 0.01 * nrm(ks[10], (nf, D), jnp.float32),
        "fnet_w_in": nrm(ks[11], (nf, D, 2 * D), jnp.float32) * D ** -0.5,
        "fnet_w_out": nrm(ks[12], (nf, D, D), jnp.float32) * D ** -0.5,
        "final_norm": 1.0 + 0.01 * nrm(ks[13], (D,), jnp.float32),
    }


def reference(x_prompt, x_sample, attn_norm, attn_w_in, attn_lambda_q1, attn_lambda_k1,
              attn_lambda_q2, attn_lambda_k2, attn_subln, attn_w_out, fnet_norm, fnet_w_in,
              fnet_w_out, final_norm):
    y_prompt = trunk(x_prompt, attn_norm, attn_w_in, attn_lambda_q1, attn_lambda_k1,
                     attn_lambda_q2, attn_lambda_k2, attn_subln, attn_w_out, fnet_norm,
                     fnet_w_in, fnet_w_out, final_norm)
    y_sample = trunk(x_sample, attn_norm, attn_w_in, attn_lambda_q1, attn_lambda_k1,
                     attn_lambda_q2, attn_lambda_k2, attn_subln, attn_w_out, fnet_norm,
                     fnet_w_in, fnet_w_out, final_norm)
    return (y_prompt, y_sample)
```

```python
import functools
import math

import numpy as np
import jax
import jax.numpy as jnp
from jax import lax
from jax.experimental import pallas as pl
from jax.experimental.pallas import tpu as pltpu

D_MODEL = 1024
DA_HEADS = 8
DA_HEAD_DIM = 64
DA_V_DIM = 128
FN_GROUPS = 8
FN_GROUP_DIM = 128
EPS = 1e-6
SUBLN_EPS = 1e-5
LAM_INIT_0 = 0.8 - 0.6 * math.exp(-0.3 * 0)

ROW_TILE = 512
ATT_TILE = 512
DFT_KB = 8
VMEM_LIMIT = 56 * 1024 * 1024

F32 = jnp.float32
BF16 = jnp.bfloat16


def _cparams(sem):
    return pltpu.CompilerParams(dimension_semantics=sem, vmem_limit_bytes=VMEM_LIMIT)


def _rms(x, g, eps):
    return x * lax.rsqrt(jnp.mean(x * x, axis=-1, keepdims=True) + eps) * g


def _silu(z):
    return z * (1.0 / (1.0 + jnp.exp(-z)))


def _dot(a, b):
    return jnp.dot(a, b, preferred_element_type=F32)


def _dot_nt(a, b):
    return lax.dot_general(a, b, (((1,), (1,)), ((), ())), preferred_element_type=F32)


def _inproj_kernel(x_ref, g_ref, wqT_ref, wk_ref, wvT_ref, wz_ref,
                   qT_ref, k_ref, vT_ref, z_ref):
    h = _rms(x_ref[...], g_ref[...], EPS).astype(BF16)
    qT_ref[0] = (_dot_nt(wqT_ref[...], h) * (DA_HEAD_DIM ** -0.5)).astype(BF16)
    vT_ref[0, 0] = _dot_nt(wvT_ref[...], h).astype(BF16)
    k_ref[...] = _dot(h, wk_ref[...]).astype(BF16)
    z_ref[...] = _dot(h, wz_ref[...])


def _inproj(x2d, g, wqT, wk, wvT, wz, B, S):
    M, D = x2d.shape
    tm = ROW_TILE
    nb = S // tm
    const = lambda i: (0, 0)
    return pl.pallas_call(
        _inproj_kernel,
        grid=(M // tm,),
        in_specs=[
            pl.BlockSpec((tm, D), lambda i: (i, 0)),
            pl.BlockSpec((1, D), const),
            pl.BlockSpec((D, D), const),
            pl.BlockSpec((D, D), const),
            pl.BlockSpec((D, D), const),
            pl.BlockSpec((D, D), const),
        ],
        out_specs=[
            pl.BlockSpec((1, D, tm), lambda i: (i // nb, 0, i % nb)),
            pl.BlockSpec((tm, D), lambda i: (i, 0)),
            pl.BlockSpec((1, 1, D, tm), lambda i: (i // nb, i % nb, 0, 0)),
            pl.BlockSpec((tm, D), lambda i: (i, 0)),
        ],
        out_shape=[
            jax.ShapeDtypeStruct((B, D, S), BF16),
            jax.ShapeDtypeStruct((M, D), BF16),
            jax.ShapeDtypeStruct((B, nb, D, tm), BF16),
            jax.ShapeDtypeStruct((M, D), F32),
        ],
        compiler_params=_cparams(("arbitrary",)),
        name="inproj",
    )(x2d, g, wqT, wk, wvT, wz)


def _attn_kernel(slopes_ref, qT_ref, k_ref, vT_ref, z_ref, sg_ref, lq1_ref, lk1_ref,
                 lq2_ref, lk2_ref, o_ref, qa_s, qb_s, m_s, l_s, acc_s, *, nk, t):
    h = pl.program_id(1)
    i = pl.program_id(2)
    slope = slopes_ref[h]

    qT = qT_ref[0]
    row = lax.broadcasted_iota(jnp.int32, qT.shape, 0)
    qa_s[...] = jnp.where(row < DA_HEAD_DIM, qT, jnp.zeros_like(qT))
    qb_s[...] = jnp.where(row >= DA_HEAD_DIM, qT, jnp.zeros_like(qT))
    m_s[...] = jnp.full(m_s.shape, -jnp.inf, F32)
    l_s[...] = jnp.zeros(l_s.shape, F32)
    acc_s[...] = jnp.zeros(acc_s.shape, F32)

    rel = (lax.broadcasted_iota(jnp.int32, (t, t), 0)
           - lax.broadcasted_iota(jnp.int32, (t, t), 1))

    def body(j, carry):
        kb = k_ref[0, pl.ds(pl.multiple_of(j * t, t), t), :]
        vT = vT_ref[0, j]
        dist = jnp.abs(rel + (j - i) * t).astype(F32)
        bias = dist * slope
        for mi, q_s in enumerate((qa_s, qb_s)):
            s = _dot(kb, q_s[...]) - bias
            m_old = m_s[mi]
            m_new = jnp.maximum(m_old, jnp.max(s, axis=0, keepdims=True))
            p = jnp.exp(s - m_new)
            alpha = jnp.exp(m_old - m_new)
            l_s[mi] = alpha * l_s[mi] + jnp.sum(p, axis=0, keepdims=True)
            acc_s[mi] = alpha * acc_s[mi] + _dot(vT, p.astype(BF16))
            m_s[mi] = m_new
        return carry

    lax.fori_loop(0, nk, body, 0)

    lam = (jnp.exp(jnp.sum(lq1_ref[...] * lk1_ref[...], axis=-1, keepdims=True))
           - jnp.exp(jnp.sum(lq2_ref[...] * lk2_ref[...], axis=-1, keepdims=True))
           + LAM_INIT_0)
    oT = acc_s[0] * (1.0 / l_s[0]) - lam * (acc_s[1] * (1.0 / l_s[1]))
    oT = oT * lax.rsqrt(jnp.mean(oT * oT, axis=0, keepdims=True) + SUBLN_EPS)
    o = oT.T * (sg_ref[...] * (1.0 - LAM_INIT_0))
    o_ref[0] = (o * _silu(z_ref[0])).astype(BF16)


def _attention(qT, k, vT, z, slopes, subln_g, lq1, lk1, lq2, lk2, B, S):
    t = ATT_TILE
    nk = S // t
    H = DA_HEADS
    dv = DA_V_DIM
    k3 = k.reshape(B, S, D_MODEL)
    z3 = z.reshape(B, S, D_MODEL)
    small = lambda b, h, i, *_: (0, 0)
    kern = functools.partial(_attn_kernel, nk=nk, t=t)
    return pl.pallas_call(
        kern,
        grid_spec=pltpu.PrefetchScalarGridSpec(
            num_scalar_prefetch=1,
            grid=(B, H, nk),
            in_specs=[
                pl.BlockSpec((1, dv, t), lambda b, h, i, *_: (b, h, i)),
                pl.BlockSpec((1, S, dv), lambda b, h, i, *_: (b, 0, h)),
                pl.BlockSpec((1, nk, dv, t), lambda b, h, i, *_: (b, 0, h, 0)),
                pl.BlockSpec((1, t, dv), lambda b, h, i, *_: (b, i, h)),
                pl.BlockSpec((1, dv), small),
                pl.BlockSpec((1, DA_HEAD_DIM), small),
                pl.BlockSpec((1, DA_HEAD_DIM), small),
                pl.BlockSpec((1, DA_HEAD_DIM), small),
                pl.BlockSpec((1, DA_HEAD_DIM), small),
            ],
            out_specs=pl.BlockSpec((1, t, dv), lambda b, h, i, *_: (b, i, h)),
            scratch_shapes=[
                pltpu.VMEM((dv, t), BF16),
                pltpu.VMEM((dv, t), BF16),
                pltpu.VMEM((2, 1, t), F32),
                pltpu.VMEM((2, 1, t), F32),
                pltpu.VMEM((2, dv, t), F32),
            ],
        ),
        out_shape=jax.ShapeDtypeStruct((B, S, D_MODEL), BF16),
        compiler_params=_cparams(("arbitrary", "arbitrary", "arbitrary")),
        name="attention",
    )(slopes, qT, k3, vT, z3, subln_g, lq1, lk1, lq2, lk2)


def _mid_kernel(og_ref, x_ref, wo_ref, g_ref, wu_ref, wz_ref, x1_ref, u_ref, z_ref):
    x1 = x_ref[...] + _dot(og_ref[...], wo_ref[...])
    x1_ref[...] = x1
    h = _rms(x1, g_ref[...], EPS).astype(BF16)
    u_ref[...] = _dot(h, wu_ref[...]).astype(BF16)
    z_ref[...] = _dot(h, wz_ref[...])


def _mid(og2d, x2d, wo, g, wu, wz):
    M, D = x2d.shape
    tm = ROW_TILE
    rows = pl.BlockSpec((tm, D), lambda i: (i, 0))
    const = lambda i: (0, 0)
    return pl.pallas_call(
        _mid_kernel,
        grid=(M // tm,),
        in_specs=[rows, rows, pl.BlockSpec((D, D), const), pl.BlockSpec((1, D), const),
                  pl.BlockSpec((D, D), const), pl.BlockSpec((D, D), const)],
        out_specs=[rows, rows, rows],
        out_shape=[jax.ShapeDtypeStruct((M, D), F32),
                   jax.ShapeDtypeStruct((M, D), BF16),
                   jax.ShapeDtypeStruct((M, D), F32)],
        compiler_params=_cparams(("arbitrary",)),
        name="mid",
    )(og2d, x2d, wo, g, wu, wz)


def _dft_factors(S):
    n2 = 1 << (int(math.log2(S)) // 2)
    return S // n2, n2


@functools.lru_cache(maxsize=None)
def _dft_tables(S):
    n1, n2 = _dft_factors(S)
    k1 = np.arange(n1, dtype=np.int64)
    t1 = np.arange(n1, dtype=np.int64)
    t2 = np.arange(n2, dtype=np.int64)
    ph = (k1[None, :, None] * (n2 * t1[None, None, :] + t2[:, None, None])) % S
    ang = 2.0 * np.pi * ph.astype(np.float64) / S
    sc1 = 1.0 / np.sqrt(n1)
    tab1 = np.concatenate([np.cos(ang), -np.sin(ang)], axis=1) * sc1
    a2 = 2.0 * np.pi * ((t2[:, None] * t2[None, :]) % n2).astype(np.float64) / n2
    c2, s2 = np.cos(a2) / np.sqrt(n2), np.sin(a2) / np.sqrt(n2)
    ga = np.concatenate([c2, -s2], axis=0)
    gb = np.concatenate([s2, c2], axis=0)
    c = np.arange(FN_GROUP_DIM, dtype=np.int64)
    a3 = 2.0 * np.pi * ((c[:, None] * c[None, :]) % FN_GROUP_DIM).astype(np.float64) / FN_GROUP_DIM
    sc3 = 1.0 / np.sqrt(FN_GROUP_DIM)
    to = lambda a: np.asarray(a, dtype=np.float32)
    return to(tab1), to(ga), to(gb), to(np.cos(a3) * sc3), to(np.sin(a3) * sc3)


def _dft1_kernel(u_ref, tab_ref, y_ref):
    y_ref[0] = _dot(tab_ref[0], u_ref[0]).astype(BF16)


def _dft1(u3, tab1, B, n1, n2):
    C = D_MODEL
    return pl.pallas_call(
        _dft1_kernel,
        grid=(B, n2),
        in_specs=[pl.BlockSpec((1, n1, C), lambda b, t: (b, 0, t)),
                  pl.BlockSpec((1, 2 * n1, n1), lambda b, t: (t, 0, 0))],
        out_specs=pl.BlockSpec((1, 2 * n1, C), lambda b, t: (b, 0, t)),
        out_shape=jax.ShapeDtypeStruct((B, 2 * n1, n2 * C), BF16),
        compiler_params=_cparams(("arbitrary", "arbitrary")),
        name="dft1",
    )(u3, tab1)


def _dft2_kernel(y_ref, ga_ref, gb_ref, c3_ref, s3_ref, f_ref, *, n2):
    for kk in range(DFT_KB):
        x = _dot(ga_ref[...], y_ref[0, 0, kk]) + _dot(gb_ref[...], y_ref[0, 1, kk])
        xr = x[:n2].astype(BF16)
        xi = x[n2:].astype(BF16)
        for g in range(FN_GROUPS):
            sl = slice(g * FN_GROUP_DIM, (g + 1) * FN_GROUP_DIM)
            f_ref[0, :, kk, sl] = _dot(xr[:, sl], c3_ref[...]) + _dot(xi[:, sl], s3_ref[...])


def _dft2(y5, ga, gb, c3, s3, B, n1, n2):
    C = D_MODEL
    const = lambda b, k: (0, 0)
    return pl.pallas_call(
        functools.partial(_dft2_kernel, n2=n2),
        grid=(B, n1 // DFT_KB),
        in_specs=[pl.BlockSpec((1, 2, DFT_KB, n2, C), lambda b, k: (b, 0, k, 0, 0)),
                  pl.BlockSpec((2 * n2, n2), const), pl.BlockSpec((2 * n2, n2), const),
                  pl.BlockSpec((FN_GROUP_DIM, FN_GROUP_DIM), const),
                  pl.BlockSpec((FN_GROUP_DIM, FN_GROUP_DIM), const)],
        out_specs=pl.BlockSpec((1, n2, DFT_KB, C), lambda b, k: (b, 0, k, 0)),
        out_shape=jax.ShapeDtypeStruct((B, n2, n1, C), F32),
        compiler_params=_cparams(("arbitrary", "arbitrary")),
        name="dft2",
    )(y5, ga, gb, c3, s3)


def _final_kernel(f_ref, z_ref, x1_ref, wo_ref, g_ref, y_ref):
    a = (f_ref[...] * _silu(z_ref[...])).astype(BF16)
    x2 = x1_ref[...] + _dot(a, wo_ref[...])
    y_ref[...] = _rms(x2, g_ref[...], EPS)


def _final(f2d, z2d, x1, wo, g):
    M, D = x1.shape
    tm = ROW_TILE
    rows = pl.BlockSpec((tm, D), lambda i: (i, 0))
    const = lambda i: (0, 0)
    return pl.pallas_call(
        _final_kernel,
        grid=(M // tm,),
        in_specs=[rows, rows, rows, pl.BlockSpec((D, D), const), pl.BlockSpec((1, D), const)],
        out_specs=rows,
        out_shape=jax.ShapeDtypeStruct((M, D), F32),
        compiler_params=_cparams(("arbitrary",)),
        name="final",
    )(f2d, z2d, x1, wo, g)


def _trunk(x, w):
    B, S, D = x.shape
    M = B * S
    x2d = x.reshape(M, D)
    qT, k, vT, z = _inproj(x2d, w["attn_norm"], w["wqT"], w["wk"], w["wvT"], w["wz"], B, S)
    og = _attention(qT, k, vT, z, w["slopes"], w["subln"], w["lq1"], w["lk1"], w["lq2"],
                    w["lk2"], B, S)
    x1, u, z2 = _mid(og.reshape(M, D), x2d, w["attn_wo"], w["fnet_norm"], w["wu"], w["wz2"])
    n1, n2 = _dft_factors(S)
    tab1, ga, gb, c3, s3 = (jnp.asarray(a).astype(BF16) for a in _dft_tables(S))
    y = _dft1(u.reshape(B, n1, n2 * D), tab1, B, n1, n2)
    f = _dft2(y.reshape(B, 2, n1, n2, D), ga, gb, c3, s3, B, n1, n2)
    out = _final(f.reshape(M, D), z2, x1, w["fnet_wo"], w["final_norm"])
    return out.reshape(B, S, D)


def kernel(x_prompt, x_sample, attn_norm, attn_w_in, attn_lambda_q1, attn_lambda_k1,
           attn_lambda_q2, attn_lambda_k2, attn_subln, attn_w_out, fnet_norm, fnet_w_in,
           fnet_w_out, final_norm):
    D = D_MODEL
    w_in = attn_w_in[0]
    w = {
        "attn_norm": attn_norm[0].reshape(1, D),
        "wqT": w_in[:, 0 * D:1 * D].T.astype(BF16),
        "wk": w_in[:, 1 * D:2 * D].astype(BF16),
        "wvT": w_in[:, 2 * D:3 * D].T.astype(BF16),
        "wz": w_in[:, 3 * D:4 * D].astype(BF16),
        "slopes": 2.0 ** (-(8.0 / DA_HEADS) * jnp.arange(1, DA_HEADS + 1, dtype=F32)),
        "subln": attn_subln[0].reshape(1, DA_V_DIM),
        "lq1": attn_lambda_q1[0].reshape(1, DA_HEAD_DIM),
        "lk1": attn_lambda_k1[0].reshape(1, DA_HEAD_DIM),
        "lq2": attn_lambda_q2[0].reshape(1, DA_HEAD_DIM),
        "lk2": attn_lambda_k2[0].reshape(1, DA_HEAD_DIM),
        "attn_wo": attn_w_out[0].astype(BF16),
        "fnet_norm": fnet_norm[0].reshape(1, D),
        "wu": fnet_w_in[0][:, :D].astype(BF16),
        "wz2": fnet_w_in[0][:, D:].astype(BF16),
        "fnet_wo": fnet_w_out[0].astype(BF16),
        "final_norm": final_norm.reshape(1, D),
    }
    return (_trunk(x_prompt, w), _trunk(x_sample, w))
```

```python
import functools
import math

import numpy as np
import jax
import jax.numpy as jnp
from jax import lax
from jax.experimental import pallas as pl
from jax.experimental.pallas import tpu as pltpu

D_MODEL = 1024
DA_HEADS = 8
DA_HEAD_DIM = 64
DA_V_DIM = 128
FN_GROUPS = 8
FN_GROUP_DIM = 128
EPS = 1e-6
SUBLN_EPS = 1e-5
LAM_INIT_0 = 0.8 - 0.6 * math.exp(-0.3 * 0)

ROW_TILE = 512
ATT_TILE = 512
DFT_KB = 8
VMEM_LIMIT = 56 * 1024 * 1024
EXP_FLUSH = 88.0
NORM_SLACK = 1.02

F32 = jnp.float32
BF16 = jnp.bfloat16


def _cparams(sem):
    return pltpu.CompilerParams(dimension_semantics=sem, vmem_limit_bytes=VMEM_LIMIT)


def _rms(x, g, eps):
    return x * lax.rsqrt(jnp.mean(x * x, axis=-1, keepdims=True) + eps) * g


def _silu(z):
    return z * (1.0 / (1.0 + jnp.exp(-z)))


def _dot(a, b):
    return jnp.dot(a, b, preferred_element_type=F32)


def _dot_nt(a, b):
    return lax.dot_general(a, b, (((1,), (1,)), ((), ())), preferred_element_type=F32)


def _inproj_kernel(x_ref, g_ref, wqT_ref, wk_ref, wvT_ref, wz_ref,
                   qT_ref, k_ref, vT_ref, z_ref):
    h = _rms(x_ref[...], g_ref[...], EPS).astype(BF16)
    qT_ref[0, 0] = (_dot_nt(wqT_ref[...], h) * (DA_HEAD_DIM ** -0.5)).astype(BF16)
    vT_ref[0, 0] = _dot_nt(wvT_ref[...], h).astype(BF16)
    k_ref[...] = _dot(h, wk_ref[...]).astype(BF16)
    z_ref[...] = _dot(h, wz_ref[...])


def _inproj(x2d, g, wqT, wk, wvT, wz, B, S):
    M, D = x2d.shape
    tm = ROW_TILE
    nb = S // tm
    const = lambda i: (0, 0)
    return pl.pallas_call(
        _inproj_kernel,
        grid=(M // tm,),
        in_specs=[
            pl.BlockSpec((tm, D), lambda i: (i, 0)),
            pl.BlockSpec((1, D), const),
            pl.BlockSpec((D, D), const),
            pl.BlockSpec((D, D), const),
            pl.BlockSpec((D, D), const),
            pl.BlockSpec((D, D), const),
        ],
        out_specs=[
            pl.BlockSpec((1, 1, D, tm), lambda i: (i // nb, i % nb, 0, 0)),
            pl.BlockSpec((tm, D), lambda i: (i, 0)),
            pl.BlockSpec((1, 1, D, tm), lambda i: (i // nb, i % nb, 0, 0)),
            pl.BlockSpec((tm, D), lambda i: (i, 0)),
        ],
        out_shape=[
            jax.ShapeDtypeStruct((B, nb, D, tm), BF16),
            jax.ShapeDtypeStruct((M, D), BF16),
            jax.ShapeDtypeStruct((B, nb, D, tm), BF16),
            jax.ShapeDtypeStruct((M, D), F32),
        ],
        compiler_params=_cparams(("arbitrary",)),
        name="inproj",
    )(x2d, g, wqT, wk, wvT, wz)


def _attn_kernel(slopes_ref, qT_ref, k_ref, vT_ref, z_ref, sg_ref, lq1_ref, lk1_ref,
                 lq2_ref, lk2_ref, o_ref, w_s, qa_s, qb_s, m_s, l_s, acc_s, *, nk, t):
    h = pl.program_id(1)
    i = pl.program_id(2)
    slope = slopes_ref[h]

    @pl.when(i == 0)
    def _():
        lane_lo = lax.broadcasted_iota(jnp.int32, (t, DA_V_DIM), 1) < DA_HEAD_DIM

        def nbody(j, c):
            k1, k2, q1, q2 = c
            kf = k_ref[0, pl.ds(pl.multiple_of(j * t, t), t), :].astype(F32)
            ksq = kf * kf
            ka = jnp.sum(jnp.where(lane_lo, ksq, 0.0), axis=1, keepdims=True)
            kb = jnp.sum(jnp.where(lane_lo, 0.0, ksq), axis=1, keepdims=True)
            qf = qT_ref[0, j].astype(F32)
            qsq = qf * qf
            qa = jnp.sum(qsq[:DA_HEAD_DIM], axis=0, keepdims=True)
            qb = jnp.sum(qsq[DA_HEAD_DIM:], axis=0, keepdims=True)
            return (jnp.maximum(k1, jnp.max(ka, axis=0, keepdims=True)),
                    jnp.maximum(k2, jnp.max(kb, axis=0, keepdims=True)),
                    jnp.maximum(q1, jnp.max(qa, axis=1, keepdims=True)),
                    jnp.maximum(q2, jnp.max(qb, axis=1, keepdims=True)))

        zero = jnp.zeros((1, 1), F32)
        k1, k2, q1, q2 = lax.fori_loop(0, nk, nbody, (zero, zero, zero, zero))
        r = jnp.sqrt(jnp.maximum(k1 * q1, k2 * q2)) * NORM_SLACK
        wf = (2.0 * r + EXP_FLUSH) / (slope * t)
        n = lax.broadcasted_iota(jnp.int32, (1, DA_V_DIM), 1) + 1
        hit = jnp.logical_and(n <= nk, n.astype(F32) <= wf)
        cnt = jnp.sum(jnp.where(hit, 1.0, 0.0), axis=1, keepdims=True)
        w_s[0] = cnt.astype(jnp.int32)[0, 0] + 2

    qT = qT_ref[0, i]
    row = lax.broadcasted_iota(jnp.int32, qT.shape, 0)
    qa_s[...] = jnp.where(row < DA_HEAD_DIM, qT, jnp.zeros_like(qT))
    qb_s[...] = jnp.where(row >= DA_HEAD_DIM, qT, jnp.zeros_like(qT))
    m_s[...] = jnp.full(m_s.shape, -jnp.inf, F32)
    l_s[...] = jnp.zeros(l_s.shape, F32)
    acc_s[...] = jnp.zeros(acc_s.shape, F32)

    rel = (lax.broadcasted_iota(jnp.int32, (t, t), 0)
           - lax.broadcasted_iota(jnp.int32, (t, t), 1))

    def body(j, carry):
        kb = k_ref[0, pl.ds(pl.multiple_of(j * t, t), t), :]
        vT = vT_ref[0, j]
        dist = jnp.abs(rel + (j - i) * t).astype(F32)
        bias = dist * slope
        for mi, q_s in enumerate((qa_s, qb_s)):
            s = _dot(kb, q_s[...]) - bias
            m_old = m_s[mi]
            m_new = jnp.maximum(m_old, jnp.max(s, axis=0, keepdims=True))
            p = jnp.exp(s - m_new)
            alpha = jnp.exp(m_old - m_new)
            l_s[mi] = alpha * l_s[mi] + jnp.sum(p, axis=0, keepdims=True)
            acc_s[mi] = alpha * acc_s[mi] + _dot(vT, p.astype(BF16))
            m_s[mi] = m_new
        return carry

    w = w_s[0]
    lax.fori_loop(jnp.maximum(i - w + 1, 0), jnp.minimum(i + w, nk), body, 0)

    lam = (jnp.exp(jnp.sum(lq1_ref[...] * lk1_ref[...], axis=-1, keepdims=True))
           - jnp.exp(jnp.sum(lq2_ref[...] * lk2_ref[...], axis=-1, keepdims=True))
           + LAM_INIT_0)
    oT = acc_s[0] * (1.0 / l_s[0]) - lam * (acc_s[1] * (1.0 / l_s[1]))
    oT = oT * lax.rsqrt(jnp.mean(oT * oT, axis=0, keepdims=True) + SUBLN_EPS)
    o = oT.T * (sg_ref[...] * (1.0 - LAM_INIT_0))
    o_ref[0] = (o * _silu(z_ref[0])).astype(BF16)


def _attention(qT, k, vT, z, slopes, subln_g, lq1, lk1, lq2, lk2, B, S):
    t = ATT_TILE
    nk = S // t
    assert S % t == 0 and nk <= DA_V_DIM
    H = DA_HEADS
    dv = DA_V_DIM
    k3 = k.reshape(B, S, D_MODEL)
    z3 = z.reshape(B, S, D_MODEL)
    small = lambda b, h, i, *_: (0, 0)
    kern = functools.partial(_attn_kernel, nk=nk, t=t)
    return pl.pallas_call(
        kern,
        grid_spec=pltpu.PrefetchScalarGridSpec(
            num_scalar_prefetch=1,
            grid=(B, H, nk),
            in_specs=[
                pl.BlockSpec((1, nk, dv, t), lambda b, h, i, *_: (b, 0, h, 0)),
                pl.BlockSpec((1, S, dv), lambda b, h, i, *_: (b, 0, h)),
                pl.BlockSpec((1, nk, dv, t), lambda b, h, i, *_: (b, 0, h, 0)),
                pl.BlockSpec((1, t, dv), lambda b, h, i, *_: (b, i, h)),
                pl.BlockSpec((1, dv), small),
                pl.BlockSpec((1, DA_HEAD_DIM), small),
                pl.BlockSpec((1, DA_HEAD_DIM), small),
                pl.BlockSpec((1, DA_HEAD_DIM), small),
                pl.BlockSpec((1, DA_HEAD_DIM), small),
            ],
            out_specs=pl.BlockSpec((1, t, dv), lambda b, h, i, *_: (b, i, h)),
            scratch_shapes=[
                pltpu.SMEM((1,), jnp.int32),
                pltpu.VMEM((dv, t), BF16),
                pltpu.VMEM((dv, t), BF16),
                pltpu.VMEM((2, 1, t), F32),
                pltpu.VMEM((2, 1, t), F32),
                pltpu.VMEM((2, dv, t), F32),
            ],
        ),
        out_shape=jax.ShapeDtypeStruct((B, S, D_MODEL), BF16),
        compiler_params=_cparams(("arbitrary", "arbitrary", "arbitrary")),
        name="attention",
    )(slopes, qT, k3, vT, z3, subln_g, lq1, lk1, lq2, lk2)


def _mid_kernel(og_ref, x_ref, wo_ref, g_ref, wu_ref, wz_ref, x1_ref, u_ref, z_ref):
    x1 = x_ref[...] + _dot(og_ref[...], wo_ref[...])
    x1_ref[...] = x1
    h = _rms(x1, g_ref[...], EPS).astype(BF16)
    u_ref[...] = _dot(h, wu_ref[...]).astype(BF16)
    z_ref[...] = _dot(h, wz_ref[...])


def _mid(og2d, x2d, wo, g, wu, wz):
    M, D = x2d.shape
    tm = ROW_TILE
    rows = pl.BlockSpec((tm, D), lambda i: (i, 0))
    const = lambda i: (0, 0)
    return pl.pallas_call(
        _mid_kernel,
        grid=(M // tm,),
        in_specs=[rows, rows, pl.BlockSpec((D, D), const), pl.BlockSpec((1, D), const),
                  pl.BlockSpec((D, D), const), pl.BlockSpec((D, D), const)],
        out_specs=[rows, rows, rows],
        out_shape=[jax.ShapeDtypeStruct((M, D), F32),
                   jax.ShapeDtypeStruct((M, D), BF16),
                   jax.ShapeDtypeStruct((M, D), F32)],
        compiler_params=_cparams(("arbitrary",)),
        name="mid",
    )(og2d, x2d, wo, g, wu, wz)


def _dft_factors(S):
    n2 = 1 << (int(math.log2(S)) // 2)
    return S // n2, n2


@functools.lru_cache(maxsize=None)
def _dft_tables(S):
    n1, n2 = _dft_factors(S)
    k1 = np.arange(n1, dtype=np.int64)
    t1 = np.arange(n1, dtype=np.int64)
    t2 = np.arange(n2, dtype=np.int64)
    ph = (k1[None, :, None] * (n2 * t1[None, None, :] + t2[:, None, None])) % S
    ang = 2.0 * np.pi * ph.astype(np.float64) / S
    sc1 = 1.0 / np.sqrt(n1)
    tab1 = np.concatenate([np.cos(ang), -np.sin(ang)], axis=1) * sc1
    a2 = 2.0 * np.pi * ((t2[:, None] * t2[None, :]) % n2).astype(np.float64) / n2
    c2, s2 = np.cos(a2) / np.sqrt(n2), np.sin(a2) / np.sqrt(n2)
    ga = np.concatenate([c2, -s2], axis=0)
    gb = np.concatenate([s2, c2], axis=0)
    c = np.arange(FN_GROUP_DIM, dtype=np.int64)
    a3 = 2.0 * np.pi * ((c[:, None] * c[None, :]) % FN_GROUP_DIM).astype(np.float64) / FN_GROUP_DIM
    sc3 = 1.0 / np.sqrt(FN_GROUP_DIM)
    to = lambda a: np.asarray(a, dtype=np.float32)
    return to(tab1), to(ga), to(gb), to(np.cos(a3) * sc3), to(np.sin(a3) * sc3)


def _dft1_kernel(u_ref, tab_ref, y_ref):
    y_ref[0] = _dot(tab_ref[0], u_ref[0]).astype(BF16)


def _dft1(u3, tab1, B, n1, n2):
    C = D_MODEL
    return pl.pallas_call(
        _dft1_kernel,
        grid=(B, n2),
        in_specs=[pl.BlockSpec((1, n1, C), lambda b, t: (b, 0, t)),
                  pl.BlockSpec((1, 2 * n1, n1), lambda b, t: (t, 0, 0))],
        out_specs=pl.BlockSpec((1, 2 * n1, C), lambda b, t: (b, 0, t)),
        out_shape=jax.ShapeDtypeStruct((B, 2 * n1, n2 * C), BF16),
        compiler_params=_cparams(("arbitrary", "arbitrary")),
        name="dft1",
    )(u3, tab1)


def _dft2_kernel(y_ref, ga_ref, gb_ref, c3_ref, s3_ref, f_ref, *, n2):
    for kk in range(DFT_KB):
        x = _dot(ga_ref[...], y_ref[0, 0, kk]) + _dot(gb_ref[...], y_ref[0, 1, kk])
        xr = x[:n2].astype(BF16)
        xi = x[n2:].astype(BF16)
        for g in range(FN_GROUPS):
            sl = slice(g * FN_GROUP_DIM, (g + 1) * FN_GROUP_DIM)
            f_ref[0, :, kk, sl] = _dot(xr[:, sl], c3_ref[...]) + _dot(xi[:, sl], s3_ref[...])


def _dft2(y5, ga, gb, c3, s3, B, n1, n2):
    C = D_MODEL
    const = lambda b, k: (0, 0)
    return pl.pallas_call(
        functools.partial(_dft2_kernel, n2=n2),
        grid=(B, n1 // DFT_KB),
        in_specs=[pl.BlockSpec((1, 2, DFT_KB, n2, C), lambda b, k: (b, 0, k, 0, 0)),
                  pl.BlockSpec((2 * n2, n2), const), pl.BlockSpec((2 * n2, n2), const),
                  pl.BlockSpec((FN_GROUP_DIM, FN_GROUP_DIM), const),
                  pl.BlockSpec((FN_GROUP_DIM, FN_GROUP_DIM), const)],
        out_specs=pl.BlockSpec((1, n2, DFT_KB, C), lambda b, k: (b, 0, k, 0)),
        out_shape=jax.ShapeDtypeStruct((B, n2, n1, C), F32),
        compiler_params=_cparams(("arbitrary", "arbitrary")),
        name="dft2",
    )(y5, ga, gb, c3, s3)


def _final_kernel(f_ref, z_ref, x1_ref, wo_ref, g_ref, y_ref):
    a = (f_ref[...] * _silu(z_ref[...])).astype(BF16)
    x2 = x1_ref[...] + _dot(a, wo_ref[...])
    y_ref[...] = _rms(x2, g_ref[...], EPS)


def _final(f2d, z2d, x1, wo, g):
    M, D = x1.shape
    tm = ROW_TILE
    rows = pl.BlockSpec((tm, D), lambda i: (i, 0))
    const = lambda i: (0, 0)
    return pl.pallas_call(
        _final_kernel,
        grid=(M // tm,),
        in_specs=[rows, rows, rows, pl.BlockSpec((D, D), const), pl.BlockSpec((1, D), const)],
        out_specs=rows,
        out_shape=jax.ShapeDtypeStruct((M, D), F32),
        compiler_params=_cparams(("arbitrary",)),
        name="final",
    )(f2d, z2d, x1, wo, g)


def _trunk(x, w):
    B, S, D = x.shape
    M = B * S
    x2d = x.reshape(M, D)
    qT, k, vT, z = _inproj(x2d, w["attn_norm"], w["wqT"], w["wk"], w["wvT"], w["wz"], B, S)
    og = _attention(qT, k, vT, z, w["slopes"], w["subln"], w["lq1"], w["lk1"], w["lq2"],
                    w["lk2"], B, S)
    x1, u, z2 = _mid(og.reshape(M, D), x2d, w["attn_wo"], w["fnet_norm"], w["wu"], w["wz2"])
    n1, n2 = _dft_factors(S)
    tab1, ga, gb, c3, s3 = (jnp.asarray(a).astype(BF16) for a in _dft_tables(S))
    y = _dft1(u.reshape(B, n1, n2 * D), tab1, B, n1, n2)
    f = _dft2(y.reshape(B, 2, n1, n2, D), ga, gb, c3, s3, B, n1, n2)
    out = _final(f.reshape(M, D), z2, x1, w["fnet_wo"], w["final_norm"])
    return out.reshape(B, S, D)


def kernel(x_prompt, x_sample, attn_norm, attn_w_in, attn_lambda_q1, attn_lambda_k1,
           attn_lambda_q2, attn_lambda_k2, attn_subln, attn_w_out, fnet_norm, fnet_w_in,
           fnet_w_out, final_norm):
    D = D_MODEL
    w_in = attn_w_in[0]
    w = {
        "attn_norm": attn_norm[0].reshape(1, D),
        "wqT": w_in[:, 0 * D:1 * D].T.astype(BF16),
        "wk": w_in[:, 1 * D:2 * D].astype(BF16),
        "wvT": w_in[:, 2 * D:3 * D].T.astype(BF16),
        "wz": w_in[:, 3 * D:4 * D].astype(BF16),
        "slopes": 2.0 ** (-(8.0 / DA_HEADS) * jnp.arange(1, DA_HEADS + 1, dtype=F32)),
        "subln": attn_subln[0].reshape(1, DA_V_DIM),
        "lq1": attn_lambda_q1[0].reshape(1, DA_HEAD_DIM),
        "lk1": attn_lambda_k1[0].reshape(1, DA_HEAD_DIM),
        "lq2": attn_lambda_q2[0].reshape(1, DA_HEAD_DIM),
        "lk2": attn_lambda_k2[0].reshape(1, DA_HEAD_DIM),
        "attn_wo": attn_w_out[0].astype(BF16),
        "fnet_norm": fnet_norm[0].reshape(1, D),
        "wu": fnet_w_in[0][:, :D].astype(BF16),
        "wz2": fnet_w_in[0][:, D:].astype(BF16),
        "fnet_wo": fnet_w_out[0].astype(BF16),
        "final_norm": final_norm.reshape(1, D),
    }
    return (_trunk(x_prompt, w), _trunk(x_sample, w))
```

```python
import functools
import math

import numpy as np
import jax
import jax.numpy as jnp
from jax import lax
from jax.experimental import pallas as pl
from jax.experimental.pallas import tpu as pltpu

D_MODEL = 1024
DA_HEADS = 8
DA_HEAD_DIM = 64
DA_V_DIM = 128
FN_GROUPS = 8
FN_GROUP_DIM = 128
EPS = 1e-6
SUBLN_EPS = 1e-5
LAM_INIT_0 = 0.8 - 0.6 * math.exp(-0.3 * 0)

ROW_TILE = 512
ATT_TILE = 512
QCHUNK = 512
DFT_KB = 8
VMEM_LIMIT = 56 * 1024 * 1024
EXP_FLUSH = 88.0
NORM_SLACK = 1.02

F32 = jnp.float32
BF16 = jnp.bfloat16


def _cparams(sem):
    return pltpu.CompilerParams(dimension_semantics=sem, vmem_limit_bytes=VMEM_LIMIT)


def _rms(x, g, eps):
    return x * lax.rsqrt(jnp.mean(x * x, axis=-1, keepdims=True) + eps) * g


def _silu(z):
    return z * (1.0 / (1.0 + jnp.exp(-z)))


def _dot(a, b):
    return jnp.dot(a, b, preferred_element_type=F32)


def _dot_nt(a, b):
    return lax.dot_general(a, b, (((1,), (1,)), ((), ())), preferred_element_type=F32)


def _inproj_kernel(x_ref, g_ref, wqT_ref, wk_ref, wvT_ref, wz_ref,
                   qT_ref, k_ref, vT_ref, z_ref):
    h = _rms(x_ref[...], g_ref[...], EPS).astype(BF16)
    qT_ref[0, 0] = (_dot_nt(wqT_ref[...], h) * (DA_HEAD_DIM ** -0.5)).astype(BF16)
    vT_ref[0, 0] = _dot_nt(wvT_ref[...], h).astype(BF16)
    k_ref[...] = _dot(h, wk_ref[...]).astype(BF16)
    z_ref[...] = _dot(h, wz_ref[...])


def _inproj(x2d, g, wqT, wk, wvT, wz, B, S):
    M, D = x2d.shape
    tm = ROW_TILE
    nb = S // tm
    const = lambda i: (0, 0)
    return pl.pallas_call(
        _inproj_kernel,
        grid=(M // tm,),
        in_specs=[
            pl.BlockSpec((tm, D), lambda i: (i, 0)),
            pl.BlockSpec((1, D), const),
            pl.BlockSpec((D, D), const),
            pl.BlockSpec((D, D), const),
            pl.BlockSpec((D, D), const),
            pl.BlockSpec((D, D), const),
        ],
        out_specs=[
            pl.BlockSpec((1, 1, D, tm), lambda i: (i // nb, i % nb, 0, 0)),
            pl.BlockSpec((tm, D), lambda i: (i, 0)),
            pl.BlockSpec((1, 1, D, tm), lambda i: (i // nb, i % nb, 0, 0)),
            pl.BlockSpec((tm, D), lambda i: (i, 0)),
        ],
        out_shape=[
            jax.ShapeDtypeStruct((B, nb, D, tm), BF16),
            jax.ShapeDtypeStruct((M, D), BF16),
            jax.ShapeDtypeStruct((B, nb, D, tm), BF16),
            jax.ShapeDtypeStruct((M, D), F32),
        ],
        compiler_params=_cparams(("arbitrary",)),
        name="inproj",
    )(x2d, g, wqT, wk, wvT, wz)


def _attn_kernel(slopes_ref, qT_ref, k_ref, vT_ref, z_ref, sg_ref, lq1_ref, lk1_ref,
                 lq2_ref, lk2_ref, o_ref, w_s, kx_s, qv_s, m_s, l_s, acc_s, *, nk, t):
    h = pl.program_id(1)
    i = pl.program_id(2)
    slope = slopes_ref[h]

    def split(x):
        hi = ((x >> 4) << 4).astype(F32) * slope
        lo = (x & 15).astype(F32) * slope
        return hi, lo

    @pl.when(i == 0)
    def _():
        lane = lax.broadcasted_iota(jnp.int32, (t, DA_V_DIM), 1)
        lane_lo = lane < DA_HEAD_DIM
        c_hi, c_lo = split(lax.broadcasted_iota(jnp.int32, (t, DA_V_DIM), 0))
        le = lane & (DA_HEAD_DIM - 1)
        k_extra = jnp.where(le == 0, c_hi, jnp.where(le == 1, c_lo,
                                                     jnp.where(le < 4, 1.0, 0.0))).astype(BF16)

        def nbody(j, c):
            k1, k2, q1, q2 = c
            rows = pl.ds(pl.multiple_of(j * t, t), t)
            kb16 = k_ref[0, rows, :]
            kx_s[0, rows, :] = jnp.where(lane_lo, kb16, k_extra)
            kx_s[1, rows, :] = jnp.where(lane_lo, k_extra, kb16)
            kf = kb16.astype(F32)
            ksq = kf * kf
            ka = jnp.sum(jnp.where(lane_lo, ksq, 0.0), axis=1, keepdims=True)
            kb = jnp.sum(jnp.where(lane_lo, 0.0, ksq), axis=1, keepdims=True)
            qf = qT_ref[0, j].astype(F32)
            qsq = qf * qf
            qa = jnp.sum(qsq[:DA_HEAD_DIM], axis=0, keepdims=True)
            qb = jnp.sum(qsq[DA_HEAD_DIM:], axis=0, keepdims=True)
            return (jnp.maximum(k1, jnp.max(ka, axis=0, keepdims=True)),
                    jnp.maximum(k2, jnp.max(kb, axis=0, keepdims=True)),
                    jnp.maximum(q1, jnp.max(qa, axis=1, keepdims=True)),
                    jnp.maximum(q2, jnp.max(qb, axis=1, keepdims=True)))

        zero = jnp.zeros((1, 1), F32)
        k1, k2, q1, q2 = lax.fori_loop(0, nk, nbody, (zero, zero, zero, zero))
        r = jnp.sqrt(jnp.maximum(k1 * q1, k2 * q2)) * NORM_SLACK
        wf = (2.0 * r + EXP_FLUSH) / (slope * t)
        n = lax.broadcasted_iota(jnp.int32, (1, DA_V_DIM), 1) + 1
        hit = jnp.logical_and(n <= nk, n.astype(F32) <= wf)
        cnt = jnp.sum(jnp.where(hit, 1.0, 0.0), axis=1, keepdims=True)
        w_s[0] = cnt.astype(jnp.int32)[0, 0] + 2

    qT = qT_ref[0, i]
    row = lax.broadcasted_iota(jnp.int32, qT.shape, 0)
    row_lo = row < DA_HEAD_DIM
    r_hi, r_lo = split(lax.broadcasted_iota(jnp.int32, qT.shape, 1))
    re = row & (DA_HEAD_DIM - 1)
    q_extra = jnp.where(re < 2, 1.0, jnp.where(re == 2, -r_hi, jnp.where(re == 3, -r_lo, 0.0)))
    for ver, ext in enumerate((q_extra, jnp.zeros_like(q_extra), -q_extra)):
        e16 = ext.astype(BF16)
        qv_s[2 * ver] = jnp.where(row_lo, qT, e16)
        qv_s[2 * ver + 1] = jnp.where(row_lo, e16, qT)
    m_s[...] = jnp.full(m_s.shape, -jnp.inf, F32)
    l_s[...] = jnp.zeros(l_s.shape, F32)
    acc_s[...] = jnp.zeros(acc_s.shape, F32)

    def step(j, ver, diag_bias):
        rows = pl.ds(pl.multiple_of(j * t, t), t)
        vT = vT_ref[0, j]
        cj = (jnp.zeros((1, QCHUNK), jnp.int32) + jnp.abs(i - j) * t).astype(F32) * (-slope)
        chains = [(mi, slice(c * QCHUNK, (c + 1) * QCHUNK))
                  for mi in range(2) for c in range(t // QCHUNK)]
        ss = [_dot(kx_s[mi, rows, :], qv_s[2 * ver + mi, :, cs]) for mi, cs in chains]
        pv, al = [], []
        for (mi, cs), s in zip(chains, ss):
            if diag_bias is not None:
                s = s - diag_bias[:, cs]
            m_old = m_s[mi, :, cs]
            m_new = jnp.maximum(m_old, jnp.max(s, axis=0, keepdims=True) + cj)
            p = jnp.exp(s - (m_new - cj))
            alpha = jnp.exp(m_old - m_new)
            l_s[mi, :, cs] = alpha * l_s[mi, :, cs] + jnp.sum(p, axis=0, keepdims=True)
            m_s[mi, :, cs] = m_new
            pv.append(_dot(vT, p.astype(BF16)))
            al.append(alpha)
        for (mi, cs), a, o in zip(chains, al, pv):
            acc_s[mi, :, cs] = a * acc_s[mi, :, cs] + o

    rel = (lax.broadcasted_iota(jnp.int32, (t, t), 0)
           - lax.broadcasted_iota(jnp.int32, (t, t), 1))
    step(i, 1, jnp.abs(rel).astype(F32) * slope)

    w = w_s[0]

    def left(j, carry):
        step(j, 0, None)
        return carry

    def right(j, carry):
        step(j, 2, None)
        return carry

    lax.fori_loop(jnp.maximum(i - w + 1, 0), i, left, 0)
    lax.fori_loop(i + 1, jnp.minimum(i + w, nk), right, 0)

    lam = (jnp.exp(jnp.sum(lq1_ref[...] * lk1_ref[...], axis=-1, keepdims=True))
           - jnp.exp(jnp.sum(lq2_ref[...] * lk2_ref[...], axis=-1, keepdims=True))
           + LAM_INIT_0)
    oT = acc_s[0] * (1.0 / l_s[0]) - lam * (acc_s[1] * (1.0 / l_s[1]))
    oT = oT * lax.rsqrt(jnp.mean(oT * oT, axis=0, keepdims=True) + SUBLN_EPS)
    o = oT.T * (sg_ref[...] * (1.0 - LAM_INIT_0))
    o_ref[0] = (o * _silu(z_ref[0])).astype(BF16)


def _attention(qT, k, vT, z, slopes, subln_g, lq1, lk1, lq2, lk2, B, S):
    t = ATT_TILE
    nk = S // t
    assert S % t == 0 and nk <= DA_V_DIM
    H = DA_HEADS
    dv = DA_V_DIM
    k3 = k.reshape(B, S, D_MODEL)
    z3 = z.reshape(B, S, D_MODEL)
    small = lambda b, h, i, *_: (0, 0)
    kern = functools.partial(_attn_kernel, nk=nk, t=t)
    return pl.pallas_call(
        kern,
        grid_spec=pltpu.PrefetchScalarGridSpec(
            num_scalar_prefetch=1,
            grid=(B, H, nk),
            in_specs=[
                pl.BlockSpec((1, nk, dv, t), lambda b, h, i, *_: (b, 0, h, 0)),
                pl.BlockSpec((1, S, dv), lambda b, h, i, *_: (b, 0, h)),
                pl.BlockSpec((1, nk, dv, t), lambda b, h, i, *_: (b, 0, h, 0)),
                pl.BlockSpec((1, t, dv), lambda b, h, i, *_: (b, i, h)),
                pl.BlockSpec((1, dv), small),
                pl.BlockSpec((1, DA_HEAD_DIM), small),
                pl.BlockSpec((1, DA_HEAD_DIM), small),
                pl.BlockSpec((1, DA_HEAD_DIM), small),
                pl.BlockSpec((1, DA_HEAD_DIM), small),
            ],
            out_specs=pl.BlockSpec((1, t, dv), lambda b, h, i, *_: (b, i, h)),
            scratch_shapes=[
                pltpu.SMEM((1,), jnp.int32),
                pltpu.VMEM((2, S, dv), BF16),
                pltpu.VMEM((6, dv, t), BF16),
                pltpu.VMEM((2, 1, t), F32),
                pltpu.VMEM((2, 1, t), F32),
                pltpu.VMEM((2, dv, t), F32),
            ],
        ),
        out_shape=jax.ShapeDtypeStruct((B, S, D_MODEL), BF16),
        compiler_params=_cparams(("arbitrary", "arbitrary", "arbitrary")),
        name="attention",
    )(slopes, qT, k3, vT, z3, subln_g, lq1, lk1, lq2, lk2)


def _mid_kernel(og_ref, x_ref, wo_ref, g_ref, wu_ref, wz_ref, x1_ref, u_ref, z_ref):
    x1 = x_ref[...] + _dot(og_ref[...], wo_ref[...])
    x1_ref[...] = x1
    h = _rms(x1, g_ref[...], EPS).astype(BF16)
    u_ref[...] = _dot(h, wu_ref[...]).astype(BF16)
    z_ref[...] = _dot(h, wz_ref[...])


def _mid(og2d, x2d, wo, g, wu, wz):
    M, D = x2d.shape
    tm = ROW_TILE
    rows = pl.BlockSpec((tm, D), lambda i: (i, 0))
    const = lambda i: (0, 0)
    return pl.pallas_call(
        _mid_kernel,
        grid=(M // tm,),
        in_specs=[rows, rows, pl.BlockSpec((D, D), const), pl.BlockSpec((1, D), const),
                  pl.BlockSpec((D, D), const), pl.BlockSpec((D, D), const)],
        out_specs=[rows, rows, rows],
        out_shape=[jax.ShapeDtypeStruct((M, D), F32),
                   jax.ShapeDtypeStruct((M, D), BF16),
                   jax.ShapeDtypeStruct((M, D), F32)],
        compiler_params=_cparams(("arbitrary",)),
        name="mid",
    )(og2d, x2d, wo, g, wu, wz)


def _dft_factors(S):
    n2 = 1 << (int(math.log2(S)) // 2)
    return S // n2, n2


@functools.lru_cache(maxsize=None)
def _dft_tables(S):
    n1, n2 = _dft_factors(S)
    k1 = np.arange(n1, dtype=np.int64)
    t1 = np.arange(n1, dtype=np.int64)
    t2 = np.arange(n2, dtype=np.int64)
    ph = (k1[None, :, None] * (n2 * t1[None, None, :] + t2[:, None, None])) % S
    ang = 2.0 * np.pi * ph.astype(np.float64) / S
    sc1 = 1.0 / np.sqrt(n1)
    tab1 = np.concatenate([np.cos(ang), -np.sin(ang)], axis=1) * sc1
    a2 = 2.0 * np.pi * ((t2[:, None] * t2[None, :]) % n2).astype(np.float64) / n2
    c2, s2 = np.cos(a2) / np.sqrt(n2), np.sin(a2) / np.sqrt(n2)
    ga = np.concatenate([c2, -s2], axis=0)
    gb = np.concatenate([s2, c2], axis=0)
    c = np.arange(FN_GROUP_DIM, dtype=np.int64)
    a3 = 2.0 * np.pi * ((c[:, None] * c[None, :]) % FN_GROUP_DIM).astype(np.float64) / FN_GROUP_DIM
    sc3 = 1.0 / np.sqrt(FN_GROUP_DIM)
    to = lambda a: np.asarray(a, dtype=np.float32)
    return to(tab1), to(ga), to(gb), to(np.cos(a3) * sc3), to(np.sin(a3) * sc3)


def _dft1_kernel(u_ref, tab_ref, y_ref):
    y_ref[0] = _dot(tab_ref[0], u_ref[0]).astype(BF16)


def _dft1(u3, tab1, B, n1, n2):
    C = D_MODEL
    return pl.pallas_call(
        _dft1_kernel,
        grid=(B, n2),
        in_specs=[pl.BlockSpec((1, n1, C), lambda b, t: (b, 0, t)),
                  pl.BlockSpec((1, 2 * n1, n1), lambda b, t: (t, 0, 0))],
        out_specs=pl.BlockSpec((1, 2 * n1, C), lambda b, t: (b, 0, t)),
        out_shape=jax.ShapeDtypeStruct((B, 2 * n1, n2 * C), BF16),
        compiler_params=_cparams(("arbitrary", "arbitrary")),
        name="dft1",
    )(u3, tab1)


def _dft2_kernel(y_ref, ga_ref, gb_ref, c3_ref, s3_ref, f_ref, *, n2):
    for kk in range(DFT_KB):
        x = _dot(ga_ref[...], y_ref[0, 0, kk]) + _dot(gb_ref[...], y_ref[0, 1, kk])
        xr = x[:n2].astype(BF16)
        xi = x[n2:].astype(BF16)
        for g in range(FN_GROUPS):
            sl = slice(g * FN_GROUP_DIM, (g + 1) * FN_GROUP_DIM)
            f_ref[0, :, kk, sl] = _dot(xr[:, sl], c3_ref[...]) + _dot(xi[:, sl], s3_ref[...])


def _dft2(y5, ga, gb, c3, s3, B, n1, n2):
    C = D_MODEL
    const = lambda b, k: (0, 0)
    return pl.pallas_call(
        functools.partial(_dft2_kernel, n2=n2),
        grid=(B, n1 // DFT_KB),
        in_specs=[pl.BlockSpec((1, 2, DFT_KB, n2, C), lambda b, k: (b, 0, k, 0, 0)),
                  pl.BlockSpec((2 * n2, n2), const), pl.BlockSpec((2 * n2, n2), const),
                  pl.BlockSpec((FN_GROUP_DIM, FN_GROUP_DIM), const),
                  pl.BlockSpec((FN_GROUP_DIM, FN_GROUP_DIM), const)],
        out_specs=pl.BlockSpec((1, n2, DFT_KB, C), lambda b, k: (b, 0, k, 0)),
        out_shape=jax.ShapeDtypeStruct((B, n2, n1, C), F32),
        compiler_params=_cparams(("arbitrary", "arbitrary")),
        name="dft2",
    )(y5, ga, gb, c3, s3)


def _final_kernel(f_ref, z_ref, x1_ref, wo_ref, g_ref, y_ref):
    a = (f_ref[...] * _silu(z_ref[...])).astype(BF16)
    x2 = x1_ref[...] + _dot(a, wo_ref[...])
    y_ref[...] = _rms(x2, g_ref[...], EPS)


def _final(f2d, z2d, x1, wo, g):
    M, D = x1.shape
    tm = ROW_TILE
    rows = pl.BlockSpec((tm, D), lambda i: (i, 0))
    const = lambda i: (0, 0)
    return pl.pallas_call(
        _final_kernel,
        grid=(M // tm,),
        in_specs=[rows, rows, rows, pl.BlockSpec((D, D), const), pl.BlockSpec((1, D), const)],
        out_specs=rows,
        out_shape=jax.ShapeDtypeStruct((M, D), F32),
        compiler_params=_cparams(("arbitrary",)),
        name="final",
    )(f2d, z2d, x1, wo, g)


def _trunk(x, w):
    B, S, D = x.shape
    M = B * S
    x2d = x.reshape(M, D)
    qT, k, vT, z = _inproj(x2d, w["attn_norm"], w["wqT"], w["wk"], w["wvT"], w["wz"], B, S)
    og = _attention(qT, k, vT, z, w["slopes"], w["subln"], w["lq1"], w["lk1"], w["lq2"],
                    w["lk2"], B, S)
    x1, u, z2 = _mid(og.reshape(M, D), x2d, w["attn_wo"], w["fnet_norm"], w["wu"], w["wz2"])
    n1, n2 = _dft_factors(S)
    tab1, ga, gb, c3, s3 = (jnp.asarray(a).astype(BF16) for a in _dft_tables(S))
    y = _dft1(u.reshape(B, n1, n2 * D), tab1, B, n1, n2)
    f = _dft2(y.reshape(B, 2, n1, n2, D), ga, gb, c3, s3, B, n1, n2)
    out = _final(f.reshape(M, D), z2, x1, w["fnet_wo"], w["final_norm"])
    return out.reshape(B, S, D)


def kernel(x_prompt, x_sample, attn_norm, attn_w_in, attn_lambda_q1, attn_lambda_k1,
           attn_lambda_q2, attn_lambda_k2, attn_subln, attn_w_out, fnet_norm, fnet_w_in,
           fnet_w_out, final_norm):
    D = D_MODEL
    w_in = attn_w_in[0]
    w = {
        "attn_norm": attn_norm[0].reshape(1, D),
        "wqT": w_in[:, 0 * D:1 * D].T.astype(BF16),
        "wk": w_in[:, 1 * D:2 * D].astype(BF16),
        "wvT": w_in[:, 2 * D:3 * D].T.astype(BF16),
        "wz": w_in[:, 3 * D:4 * D].astype(BF16),
        "slopes": 2.0 ** (-(8.0 / DA_HEADS) * jnp.arange(1, DA_HEADS + 1, dtype=F32)),
        "subln": attn_subln[0].reshape(1, DA_V_DIM),
        "lq1": attn_lambda_q1[0].reshape(1, DA_HEAD_DIM),
        "lk1": attn_lambda_k1[0].reshape(1, DA_HEAD_DIM),
        "lq2": attn_lambda_q2[0].reshape(1, DA_HEAD_DIM),
        "lk2": attn_lambda_k2[0].reshape(1, DA_HEAD_DIM),
        "attn_wo": attn_w_out[0].astype(BF16),
        "fnet_norm": fnet_norm[0].reshape(1, D),
        "wu": fnet_w_in[0][:, :D].astype(BF16),
        "wz2": fnet_w_in[0][:, D:].astype(BF16),
        "fnet_wo": fnet_w_out[0].astype(BF16),
        "final_norm": final_norm.reshape(1, D),
    }
    return (_trunk(x_prompt, w), _trunk(x_sample, w))
```

```python
import functools
import math

import numpy as np
import jax
import jax.numpy as jnp
from jax import lax
from jax.experimental import pallas as pl
from jax.experimental.pallas import tpu as pltpu

D_MODEL = 1024
DA_HEADS = 8
DA_HEAD_DIM = 64
DA_V_DIM = 128
FN_GROUPS = 8
FN_GROUP_DIM = 128
EPS = 1e-6
SUBLN_EPS = 1e-5
LAM_INIT_0 = 0.8 - 0.6 * math.exp(-0.3 * 0)

ROW_TILE = 512
ATT_TILE = 512
DFT_KB = 8
VMEM_LIMIT = 56 * 1024 * 1024
EXP_FLUSH = 88.0
NORM_SLACK = 1.02

F32 = jnp.float32
BF16 = jnp.bfloat16


def _cparams(sem):
    return pltpu.CompilerParams(dimension_semantics=sem, vmem_limit_bytes=VMEM_LIMIT)


def _rms(x, g, eps):
    return x * lax.rsqrt(jnp.mean(x * x, axis=-1, keepdims=True) + eps) * g


def _silu(z):
    return z * (1.0 / (1.0 + jnp.exp(-z)))


def _dot(a, b):
    return jnp.dot(a, b, preferred_element_type=F32)


def _dot_nt(a, b):
    return lax.dot_general(a, b, (((1,), (1,)), ((), ())), preferred_element_type=F32)


def _inproj_kernel(x_ref, g_ref, wqT_ref, wk_ref, wvT_ref, wz_ref,
                   qT_ref, k_ref, vT_ref, z_ref):
    h = _rms(x_ref[...], g_ref[...], EPS).astype(BF16)
    qT_ref[0, 0] = (_dot_nt(wqT_ref[...], h) * (DA_HEAD_DIM ** -0.5)).astype(BF16)
    vT_ref[0, 0] = _dot_nt(wvT_ref[...], h).astype(BF16)
    k_ref[...] = _dot(h, wk_ref[...]).astype(BF16)
    z_ref[...] = _dot(h, wz_ref[...])


def _inproj(x2d, g, wqT, wk, wvT, wz, B, S):
    M, D = x2d.shape
    tm = ROW_TILE
    nb = S // tm
    const = lambda i: (0, 0)
    return pl.pallas_call(
        _inproj_kernel,
        grid=(M // tm,),
        in_specs=[
            pl.BlockSpec((tm, D), lambda i: (i, 0)),
            pl.BlockSpec((1, D), const),
            pl.BlockSpec((D, D), const),
            pl.BlockSpec((D, D), const),
            pl.BlockSpec((D, D), const),
            pl.BlockSpec((D, D), const),
        ],
        out_specs=[
            pl.BlockSpec((1, 1, D, tm), lambda i: (i // nb, i % nb, 0, 0)),
            pl.BlockSpec((tm, D), lambda i: (i, 0)),
            pl.BlockSpec((1, 1, D, tm), lambda i: (i // nb, i % nb, 0, 0)),
            pl.BlockSpec((tm, D), lambda i: (i, 0)),
        ],
        out_shape=[
            jax.ShapeDtypeStruct((B, nb, D, tm), BF16),
            jax.ShapeDtypeStruct((M, D), BF16),
            jax.ShapeDtypeStruct((B, nb, D, tm), BF16),
            jax.ShapeDtypeStruct((M, D), F32),
        ],
        compiler_params=_cparams(("arbitrary",)),
        name="inproj",
    )(x2d, g, wqT, wk, wvT, wz)


def _attn_kernel(slopes_ref, qT_ref, k_ref, vT_ref, z_ref, sg_ref, lq1_ref, lk1_ref,
                 lq2_ref, lk2_ref, o_ref, w_s, kx_s, qv_s, m_s, l_s, acc_s, *, nk, t):
    h = pl.program_id(1)
    i = pl.program_id(2)
    slope = slopes_ref[h]

    def split(x):
        hi = ((x >> 4) << 4).astype(F32) * slope
        lo = (x & 15).astype(F32) * slope
        return hi, lo

    @pl.when(i == 0)
    def _():
        lane = lax.broadcasted_iota(jnp.int32, (t, DA_V_DIM), 1)
        lane_lo = lane < DA_HEAD_DIM
        c_hi, c_lo = split(lax.broadcasted_iota(jnp.int32, (t, DA_V_DIM), 0))
        le = lane & (DA_HEAD_DIM - 1)
        k_extra = jnp.where(le == 0, c_hi, jnp.where(le == 1, c_lo,
                                                     jnp.where(le < 4, 1.0, 0.0))).astype(BF16)

        def nbody(j, c):
            k1, k2, q1, q2 = c
            rows = pl.ds(pl.multiple_of(j * t, t), t)
            kb16 = k_ref[0, rows, :]
            kx_s[0, rows, :] = jnp.where(lane_lo, kb16, k_extra)
            kx_s[1, rows, :] = jnp.where(lane_lo, k_extra, kb16)
            kf = kb16.astype(F32)
            ksq = kf * kf
            ka = jnp.sum(jnp.where(lane_lo, ksq, 0.0), axis=1, keepdims=True)
            kb = jnp.sum(jnp.where(lane_lo, 0.0, ksq), axis=1, keepdims=True)
            qf = qT_ref[0, j].astype(F32)
            qsq = qf * qf
            qa = jnp.sum(qsq[:DA_HEAD_DIM], axis=0, keepdims=True)
            qb = jnp.sum(qsq[DA_HEAD_DIM:], axis=0, keepdims=True)
            return (jnp.maximum(k1, jnp.max(ka, axis=0, keepdims=True)),
                    jnp.maximum(k2, jnp.max(kb, axis=0, keepdims=True)),
                    jnp.maximum(q1, jnp.max(qa, axis=1, keepdims=True)),
                    jnp.maximum(q2, jnp.max(qb, axis=1, keepdims=True)))

        zero = jnp.zeros((1, 1), F32)
        k1, k2, q1, q2 = lax.fori_loop(0, nk, nbody, (zero, zero, zero, zero))
        r = jnp.sqrt(jnp.maximum(k1 * q1, k2 * q2)) * NORM_SLACK
        wf = (2.0 * r + EXP_FLUSH) / (slope * t)
        n = lax.broadcasted_iota(jnp.int32, (1, DA_V_DIM), 1) + 1
        hit = jnp.logical_and(n <= nk, n.astype(F32) <= wf)
        cnt = jnp.sum(jnp.where(hit, 1.0, 0.0), axis=1, keepdims=True)
        w_s[0] = cnt.astype(jnp.int32)[0, 0] + 2

    qT = qT_ref[0, i]
    row = lax.broadcasted_iota(jnp.int32, qT.shape, 0)
    row_lo = row < DA_HEAD_DIM
    r_hi, r_lo = split(lax.broadcasted_iota(jnp.int32, qT.shape, 1))
    re = row & (DA_HEAD_DIM - 1)
    q_extra = jnp.where(re < 2, 1.0, jnp.where(re == 2, -r_hi, jnp.where(re == 3, -r_lo, 0.0)))
    for ver, ext in enumerate((q_extra, jnp.zeros_like(q_extra), -q_extra)):
        e16 = ext.astype(BF16)
        qv_s[2 * ver] = jnp.where(row_lo, qT, e16)
        qv_s[2 * ver + 1] = jnp.where(row_lo, e16, qT)
    m_s[...] = jnp.full(m_s.shape, -jnp.inf, F32)
    l_s[...] = jnp.zeros(l_s.shape, F32)
    acc_s[...] = jnp.zeros(acc_s.shape, F32)

    def step(js, ver, diag_bias=None):
        rows = [pl.ds(pl.multiple_of(j * t, t), t) for j in js]
        cjs = [(jnp.zeros((1, t), jnp.int32) + jnp.abs(i - j) * t).astype(F32) * (-slope)
               for j in js]
        ss = [[_dot(kx_s[mi, r, :], qv_s[2 * ver + mi]) for r in rows] for mi in range(2)]
        pv, al = [], []
        for mi in range(2):
            sm = ss[mi] if diag_bias is None else [s - diag_bias for s in ss[mi]]
            m_old = m_s[mi]
            m_new = m_old
            for s, cj in zip(sm, cjs):
                m_new = jnp.maximum(m_new, jnp.max(s, axis=0, keepdims=True) + cj)
            alpha = jnp.exp(m_old - m_new)
            l_new = alpha * l_s[mi]
            o = None
            for s, cj, j in zip(sm, cjs, js):
                p = jnp.exp(s - (m_new - cj))
                l_new = l_new + jnp.sum(p, axis=0, keepdims=True)
                d = _dot(vT_ref[0, j], p.astype(BF16))
                o = d if o is None else o + d
            l_s[mi] = l_new
            m_s[mi] = m_new
            pv.append(o)
            al.append(alpha)
        for mi in range(2):
            acc_s[mi] = al[mi] * acc_s[mi] + pv[mi]

    rel = (lax.broadcasted_iota(jnp.int32, (t, t), 0)
           - lax.broadcasted_iota(jnp.int32, (t, t), 1))
    step([i], 1, jnp.abs(rel).astype(F32) * slope)

    w = w_s[0]

    def sweep(lo, hi, ver):
        n = hi - lo

        def pair(q, carry):
            step([lo + 2 * q, lo + 2 * q + 1], ver)
            return carry

        lax.fori_loop(0, n >> 1, pair, 0)

        @pl.when((n & 1) == 1)
        def _():
            step([hi - 1], ver)

    sweep(jnp.maximum(i - w + 1, 0), i, 0)
    sweep(i + 1, jnp.minimum(i + w, nk), 2)

    lam = (jnp.exp(jnp.sum(lq1_ref[...] * lk1_ref[...], axis=-1, keepdims=True))
           - jnp.exp(jnp.sum(lq2_ref[...] * lk2_ref[...], axis=-1, keepdims=True))
           + LAM_INIT_0)
    oT = acc_s[0] * (1.0 / l_s[0]) - lam * (acc_s[1] * (1.0 / l_s[1]))
    oT = oT * lax.rsqrt(jnp.mean(oT * oT, axis=0, keepdims=True) + SUBLN_EPS)
    o = oT.T * (sg_ref[...] * (1.0 - LAM_INIT_0))
    o_ref[0] = (o * _silu(z_ref[0])).astype(BF16)


def _attention(qT, k, vT, z, slopes, subln_g, lq1, lk1, lq2, lk2, B, S):
    t = ATT_TILE
    nk = S // t
    assert S % t == 0 and nk <= DA_V_DIM
    H = DA_HEADS
    dv = DA_V_DIM
    k3 = k.reshape(B, S, D_MODEL)
    z3 = z.reshape(B, S, D_MODEL)
    small = lambda b, h, i, *_: (0, 0)
    kern = functools.partial(_attn_kernel, nk=nk, t=t)
    return pl.pallas_call(
        kern,
        grid_spec=pltpu.PrefetchScalarGridSpec(
            num_scalar_prefetch=1,
            grid=(B, H, nk),
            in_specs=[
                pl.BlockSpec((1, nk, dv, t), lambda b, h, i, *_: (b, 0, h, 0)),
                pl.BlockSpec((1, S, dv), lambda b, h, i, *_: (b, 0, h)),
                pl.BlockSpec((1, nk, dv, t), lambda b, h, i, *_: (b, 0, h, 0)),
                pl.BlockSpec((1, t, dv), lambda b, h, i, *_: (b, i, h)),
                pl.BlockSpec((1, dv), small),
                pl.BlockSpec((1, DA_HEAD_DIM), small),
                pl.BlockSpec((1, DA_HEAD_DIM), small),
                pl.BlockSpec((1, DA_HEAD_DIM), small),
                pl.BlockSpec((1, DA_HEAD_DIM), small),
            ],
            out_specs=pl.BlockSpec((1, t, dv), lambda b, h, i, *_: (b, i, h)),
            scratch_shapes=[
                pltpu.SMEM((1,), jnp.int32),
                pltpu.VMEM((2, S, dv), BF16),
                pltpu.VMEM((6, dv, t), BF16),
                pltpu.VMEM((2, 1, t), F32),
                pltpu.VMEM((2, 1, t), F32),
                pltpu.VMEM((2, dv, t), F32),
            ],
        ),
        out_shape=jax.ShapeDtypeStruct((B, S, D_MODEL), BF16),
        compiler_params=_cparams(("arbitrary", "arbitrary", "arbitrary")),
        name="attention",
    )(slopes, qT, k3, vT, z3, subln_g, lq1, lk1, lq2, lk2)


def _mid_kernel(og_ref, x_ref, wo_ref, g_ref, wu_ref, wz_ref, x1_ref, u_ref, z_ref):
    x1 = x_ref[...] + _dot(og_ref[...], wo_ref[...])
    x1_ref[...] = x1
    h = _rms(x1, g_ref[...], EPS).astype(BF16)
    u_ref[...] = _dot(h, wu_ref[...]).astype(BF16)
    z_ref[...] = _dot(h, wz_ref[...])


def _mid(og2d, x2d, wo, g, wu, wz):
    M, D = x2d.shape
    tm = ROW_TILE
    rows = pl.BlockSpec((tm, D), lambda i: (i, 0))
    const = lambda i: (0, 0)
    return pl.pallas_call(
        _mid_kernel,
        grid=(M // tm,),
        in_specs=[rows, rows, pl.BlockSpec((D, D), const), pl.BlockSpec((1, D), const),
                  pl.BlockSpec((D, D), const), pl.BlockSpec((D, D), const)],
        out_specs=[rows, rows, rows],
        out_shape=[jax.ShapeDtypeStruct((M, D), F32),
                   jax.ShapeDtypeStruct((M, D), BF16),
                   jax.ShapeDtypeStruct((M, D), F32)],
        compiler_params=_cparams(("arbitrary",)),
        name="mid",
    )(og2d, x2d, wo, g, wu, wz)


def _dft_factors(S):
    n2 = 1 << (int(math.log2(S)) // 2)
    return S // n2, n2


@functools.lru_cache(maxsize=None)
def _dft_tables(S):
    n1, n2 = _dft_factors(S)
    k1 = np.arange(n1, dtype=np.int64)
    t1 = np.arange(n1, dtype=np.int64)
    t2 = np.arange(n2, dtype=np.int64)
    ph = (k1[None, :, None] * (n2 * t1[None, None, :] + t2[:, None, None])) % S
    ang = 2.0 * np.pi * ph.astype(np.float64) / S
    sc1 = 1.0 / np.sqrt(n1)
    tab1 = np.concatenate([np.cos(ang), -np.sin(ang)], axis=1) * sc1
    a2 = 2.0 * np.pi * ((t2[:, None] * t2[None, :]) % n2).astype(np.float64) / n2
    c2, s2 = np.cos(a2) / np.sqrt(n2), np.sin(a2) / np.sqrt(n2)
    ga = np.concatenate([c2, -s2], axis=0)
    gb = np.concatenate([s2, c2], axis=0)
    c = np.arange(FN_GROUP_DIM, dtype=np.int64)
    a3 = 2.0 * np.pi * ((c[:, None] * c[None, :]) % FN_GROUP_DIM).astype(np.float64) / FN_GROUP_DIM
    sc3 = 1.0 / np.sqrt(FN_GROUP_DIM)
    to = lambda a: np.asarray(a, dtype=np.float32)
    return to(tab1), to(ga), to(gb), to(np.cos(a3) * sc3), to(np.sin(a3) * sc3)


def _dft1_kernel(u_ref, tab_ref, y_ref):
    y_ref[0] = _dot(tab_ref[0], u_ref[0]).astype(BF16)


def _dft1(u3, tab1, B, n1, n2):
    C = D_MODEL
    return pl.pallas_call(
        _dft1_kernel,
        grid=(B, n2),
        in_specs=[pl.BlockSpec((1, n1, C), lambda b, t: (b, 0, t)),
                  pl.BlockSpec((1, 2 * n1, n1), lambda b, t: (t, 0, 0))],
        out_specs=pl.BlockSpec((1, 2 * n1, C), lambda b, t: (b, 0, t)),
        out_shape=jax.ShapeDtypeStruct((B, 2 * n1, n2 * C), BF16),
        compiler_params=_cparams(("arbitrary", "arbitrary")),
        name="dft1",
    )(u3, tab1)


def _dft2_kernel(y_ref, ga_ref, gb_ref, c3_ref, s3_ref, f_ref, *, n2):
    for kk in range(DFT_KB):
        x = _dot(ga_ref[...], y_ref[0, 0, kk]) + _dot(gb_ref[...], y_ref[0, 1, kk])
        xr = x[:n2].astype(BF16)
        xi = x[n2:].astype(BF16)
        for g in range(FN_GROUPS):
            sl = slice(g * FN_GROUP_DIM, (g + 1) * FN_GROUP_DIM)
            f_ref[0, :, kk, sl] = _dot(xr[:, sl], c3_ref[...]) + _dot(xi[:, sl], s3_ref[...])


def _dft2(y5, ga, gb, c3, s3, B, n1, n2):
    C = D_MODEL
    const = lambda b, k: (0, 0)
    return pl.pallas_call(
        functools.partial(_dft2_kernel, n2=n2),
        grid=(B, n1 // DFT_KB),
        in_specs=[pl.BlockSpec((1, 2, DFT_KB, n2, C), lambda b, k: (b, 0, k, 0, 0)),
                  pl.BlockSpec((2 * n2, n2), const), pl.BlockSpec((2 * n2, n2), const),
                  pl.BlockSpec((FN_GROUP_DIM, FN_GROUP_DIM), const),
                  pl.BlockSpec((FN_GROUP_DIM, FN_GROUP_DIM), const)],
        out_specs=pl.BlockSpec((1, n2, DFT_KB, C), lambda b, k: (b, 0, k, 0)),
        out_shape=jax.ShapeDtypeStruct((B, n2, n1, C), F32),
        compiler_params=_cparams(("arbitrary", "arbitrary")),
        name="dft2",
    )(y5, ga, gb, c3, s3)


def _final_kernel(f_ref, z_ref, x1_ref, wo_ref, g_ref, y_ref):
    a = (f_ref[...] * _silu(z_ref[...])).astype(BF16)
    x2 = x1_ref[...] + _dot(a, wo_ref[...])
    y_ref[...] = _rms(x2, g_ref[...], EPS)


def _final(f2d, z2d, x1, wo, g):
    M, D = x1.shape
    tm = ROW_TILE
    rows = pl.BlockSpec((tm, D), lambda i: (i, 0))
    const = lambda i: (0, 0)
    return pl.pallas_call(
        _final_kernel,
        grid=(M // tm,),
        in_specs=[rows, rows, rows, pl.BlockSpec((D, D), const), pl.BlockSpec((1, D), const)],
        out_specs=rows,
        out_shape=jax.ShapeDtypeStruct((M, D), F32),
        compiler_params=_cparams(("arbitrary",)),
        name="final",
    )(f2d, z2d, x1, wo, g)


def _trunk(x, w):
    B, S, D = x.shape
    M = B * S
    x2d = x.reshape(M, D)
    qT, k, vT, z = _inproj(x2d, w["attn_norm"], w["wqT"], w["wk"], w["wvT"], w["wz"], B, S)
    og = _attention(qT, k, vT, z, w["slopes"], w["subln"], w["lq1"], w["lk1"], w["lq2"],
                    w["lk2"], B, S)
    x1, u, z2 = _mid(og.reshape(M, D), x2d, w["attn_wo"], w["fnet_norm"], w["wu"], w["wz2"])
    n1, n2 = _dft_factors(S)
    tab1, ga, gb, c3, s3 = (jnp.asarray(a).astype(BF16) for a in _dft_tables(S))
    y = _dft1(u.reshape(B, n1, n2 * D), tab1, B, n1, n2)
    f = _dft2(y.reshape(B, 2, n1, n2, D), ga, gb, c3, s3, B, n1, n2)
    out = _final(f.reshape(M, D), z2, x1, w["fnet_wo"], w["final_norm"])
    return out.reshape(B, S, D)


def kernel(x_prompt, x_sample, attn_norm, attn_w_in, attn_lambda_q1, attn_lambda_k1,
           attn_lambda_q2, attn_lambda_k2, attn_subln, attn_w_out, fnet_norm, fnet_w_in,
           fnet_w_out, final_norm):
    D = D_MODEL
    w_in = attn_w_in[0]
    w = {
        "attn_norm": attn_norm[0].reshape(1, D),
        "wqT": w_in[:, 0 * D:1 * D].T.astype(BF16),
        "wk": w_in[:, 1 * D:2 * D].astype(BF16),
        "wvT": w_in[:, 2 * D:3 * D].T.astype(BF16),
        "wz": w_in[:, 3 * D:4 * D].astype(BF16),
        "slopes": 2.0 ** (-(8.0 / DA_HEADS) * jnp.arange(1, DA_HEADS + 1, dtype=F32)),
        "subln": attn_subln[0].reshape(1, DA_V_DIM),
        "lq1": attn_lambda_q1[0].reshape(1, DA_HEAD_DIM),
        "lk1": attn_lambda_k1[0].reshape(1, DA_HEAD_DIM),
        "lq2": attn_lambda_q2[0].reshape(1, DA_HEAD_DIM),
        "lk2": attn_lambda_k2[0].reshape(1, DA_HEAD_DIM),
        "attn_wo": attn_w_out[0].astype(BF16),
        "fnet_norm": fnet_norm[0].reshape(1, D),
        "wu": fnet_w_in[0][:, :D].astype(BF16),
        "wz2": fnet_w_in[0][:, D:].astype(BF16),
        "fnet_wo": fnet_w_out[0].astype(BF16),
        "final_norm": final_norm.reshape(1, D),
    }
    return (_trunk(x_prompt, w), _trunk(x_sample, w))
```

```python
import functools
import math

import numpy as np
import jax
import jax.numpy as jnp
from jax import lax
from jax.experimental import pallas as pl
from jax.experimental.pallas import tpu as pltpu

D_MODEL = 1024
DA_HEADS = 8
DA_HEAD_DIM = 64
DA_V_DIM = 128
FN_GROUPS = 8
FN_GROUP_DIM = 128
EPS = 1e-6
SUBLN_EPS = 1e-5
LAM_INIT_0 = 0.8 - 0.6 * math.exp(-0.3 * 0)

ROW_TILE = 512
ATT_TILE = 512
DFT_KB = 8
VMEM_LIMIT = 56 * 1024 * 1024
EXP_FLUSH = 88.0
NORM_SLACK = 1.02

F32 = jnp.float32
BF16 = jnp.bfloat16


def _cparams(sem):
    return pltpu.CompilerParams(dimension_semantics=sem, vmem_limit_bytes=VMEM_LIMIT)


def _rms(x, g, eps):
    return x * lax.rsqrt(jnp.mean(x * x, axis=-1, keepdims=True) + eps) * g


def _silu(z):
    return z * (1.0 / (1.0 + jnp.exp(-z)))


def _dot(a, b):
    return jnp.dot(a, b, preferred_element_type=F32)


def _dot_nt(a, b):
    return lax.dot_general(a, b, (((1,), (1,)), ((), ())), preferred_element_type=F32)


def _inproj_kernel(x_ref, g_ref, wqT_ref, wk_ref, wvT_ref, wz_ref,
                   qT_ref, k_ref, vT_ref, z_ref):
    h = _rms(x_ref[...], g_ref[...], EPS).astype(BF16)
    qT_ref[0, 0] = (_dot_nt(wqT_ref[...], h) * (DA_HEAD_DIM ** -0.5)).astype(BF16)
    vT_ref[0, 0] = _dot_nt(wvT_ref[...], h).astype(BF16)
    k_ref[...] = _dot(h, wk_ref[...]).astype(BF16)
    z_ref[...] = _dot(h, wz_ref[...])


def _inproj(x2d, g, wqT, wk, wvT, wz, B, S):
    M, D = x2d.shape
    tm = ROW_TILE
    nb = S // tm
    const = lambda i: (0, 0)
    return pl.pallas_call(
        _inproj_kernel,
        grid=(M // tm,),
        in_specs=[
            pl.BlockSpec((tm, D), lambda i: (i, 0)),
            pl.BlockSpec((1, D), const),
            pl.BlockSpec((D, D), const),
            pl.BlockSpec((D, D), const),
            pl.BlockSpec((D, D), const),
            pl.BlockSpec((D, D), const),
        ],
        out_specs=[
            pl.BlockSpec((1, 1, D, tm), lambda i: (i // nb, i % nb, 0, 0)),
            pl.BlockSpec((tm, D), lambda i: (i, 0)),
            pl.BlockSpec((1, 1, D, tm), lambda i: (i // nb, i % nb, 0, 0)),
            pl.BlockSpec((tm, D), lambda i: (i, 0)),
        ],
        out_shape=[
            jax.ShapeDtypeStruct((B, nb, D, tm), BF16),
            jax.ShapeDtypeStruct((M, D), BF16),
            jax.ShapeDtypeStruct((B, nb, D, tm), BF16),
            jax.ShapeDtypeStruct((M, D), F32),
        ],
        compiler_params=_cparams(("arbitrary",)),
        name="inproj",
    )(x2d, g, wqT, wk, wvT, wz)


def _attn_kernel(slopes_ref, qT_ref, k_ref, vT_ref, z_ref, sg_ref, lq1_ref, lk1_ref,
                 lq2_ref, lk2_ref, o_ref, w_s, kx_s, qv_s, m_s, l_s, acc_s, *, nk, t):
    h = pl.program_id(1)
    i = pl.program_id(2)
    slope = slopes_ref[h]

    def split(x):
        hi = ((x >> 4) << 4).astype(F32) * slope
        lo = (x & 15).astype(F32) * slope
        return hi, lo

    @pl.when(i == 0)
    def _():
        lane = lax.broadcasted_iota(jnp.int32, (t, DA_V_DIM), 1)
        lane_lo = lane < DA_HEAD_DIM
        c_hi, c_lo = split(lax.broadcasted_iota(jnp.int32, (t, DA_V_DIM), 0))
        le = lane & (DA_HEAD_DIM - 1)
        k_extra = jnp.where(le == 0, c_hi, jnp.where(le == 1, c_lo,
                                                     jnp.where(le < 4, 1.0, 0.0))).astype(BF16)

        def nbody(j, c):
            k1, k2, q1, q2 = c
            rows = pl.ds(pl.multiple_of(j * t, t), t)
            kb16 = k_ref[0, rows, :]
            kx_s[0, rows, :] = jnp.where(lane_lo, kb16, k_extra)
            kx_s[1, rows, :] = jnp.where(lane_lo, k_extra, kb16)
            kf = kb16.astype(F32)
            ksq = kf * kf
            ka = jnp.sum(jnp.where(lane_lo, ksq, 0.0), axis=1, keepdims=True)
            kb = jnp.sum(jnp.where(lane_lo, 0.0, ksq), axis=1, keepdims=True)
            qf = qT_ref[0, j].astype(F32)
            qsq = qf * qf
            qa = jnp.sum(qsq[:DA_HEAD_DIM], axis=0, keepdims=True)
            qb = jnp.sum(qsq[DA_HEAD_DIM:], axis=0, keepdims=True)
            return (jnp.maximum(k1, jnp.max(ka, axis=0, keepdims=True)),
                    jnp.maximum(k2, jnp.max(kb, axis=0, keepdims=True)),
                    jnp.maximum(q1, jnp.max(qa, axis=1, keepdims=True)),
                    jnp.maximum(q2, jnp.max(qb, axis=1, keepdims=True)))

        zero = jnp.zeros((1, 1), F32)
        k1, k2, q1, q2 = lax.fori_loop(0, nk, nbody, (zero, zero, zero, zero))
        r = jnp.sqrt(jnp.maximum(k1 * q1, k2 * q2)) * NORM_SLACK
        wf = (2.0 * r + EXP_FLUSH) / (slope * t)
        n = lax.broadcasted_iota(jnp.int32, (1, DA_V_DIM), 1) + 1
        hit = jnp.logical_and(n <= nk, n.astype(F32) <= wf)
        cnt = jnp.sum(jnp.where(hit, 1.0, 0.0), axis=1, keepdims=True)
        w_s[0] = cnt.astype(jnp.int32)[0, 0] + 2

    qT = qT_ref[0, i]
    row = lax.broadcasted_iota(jnp.int32, qT.shape, 0)
    row_lo = row < DA_HEAD_DIM
    r_hi, r_lo = split(lax.broadcasted_iota(jnp.int32, qT.shape, 1))
    re = row & (DA_HEAD_DIM - 1)
    q_extra = jnp.where(re < 2, 1.0, jnp.where(re == 2, -r_hi, jnp.where(re == 3, -r_lo, 0.0)))
    for ver, ext in enumerate((q_extra, jnp.zeros_like(q_extra), -q_extra)):
        e16 = ext.astype(BF16)
        qv_s[2 * ver] = jnp.where(row_lo, qT, e16)
        qv_s[2 * ver + 1] = jnp.where(row_lo, e16, qT)
    m_s[...] = jnp.full(m_s.shape, -jnp.inf, F32)
    l_s[...] = jnp.zeros(l_s.shape, F32)
    acc_s[...] = jnp.zeros(acc_s.shape, F32)

    def step(js, ver, diag_bias=None):
        rows = [pl.ds(pl.multiple_of(j * t, t), t) for j in js]
        cjs = [(jnp.zeros((1, t), jnp.int32) + jnp.abs(i - j) * t).astype(F32) * (-slope)
               for j in js]
        ss = [[_dot(kx_s[mi, r, :], qv_s[2 * ver + mi]) for r in rows] for mi in range(2)]
        pv, al = [], []
        for mi in range(2):
            sm = ss[mi] if diag_bias is None else [s - diag_bias for s in ss[mi]]
            m_old = m_s[mi]
            m_new = m_old
            for s, cj in zip(sm, cjs):
                m_new = jnp.maximum(m_new, jnp.max(s, axis=0, keepdims=True) + cj)
            alpha = jnp.exp(m_old - m_new)
            l_new = alpha * l_s[mi]
            o = None
            for s, cj, j in zip(sm, cjs, js):
                p = jnp.exp(s - (m_new - cj))
                l_new = l_new + jnp.sum(p, axis=0, keepdims=True)
                d = _dot(vT_ref[0, j], p.astype(BF16))
                o = d if o is None else o + d
            l_s[mi] = l_new
            m_s[mi] = m_new
            pv.append(o)
            al.append(alpha)
        for mi in range(2):
            acc_s[mi] = al[mi] * acc_s[mi] + pv[mi]

    rel = (lax.broadcasted_iota(jnp.int32, (t, t), 0)
           - lax.broadcasted_iota(jnp.int32, (t, t), 1))
    step([i], 1, jnp.abs(rel).astype(F32) * slope)

    w = w_s[0]

    def sweep(lo, hi, ver):
        n = hi - lo

        def quad(q, carry):
            step([lo + 4 * q + d for d in range(4)], ver)
            return carry

        lax.fori_loop(0, n >> 2, quad, 0)
        rest = lo + ((n >> 2) << 2)

        @pl.when((n & 2) == 2)
        def _():
            step([rest, rest + 1], ver)

        @pl.when((n & 1) == 1)
        def _():
            step([hi - 1], ver)

    sweep(jnp.maximum(i - w + 1, 0), i, 0)
    sweep(i + 1, jnp.minimum(i + w, nk), 2)

    lam = (jnp.exp(jnp.sum(lq1_ref[...] * lk1_ref[...], axis=-1, keepdims=True))
           - jnp.exp(jnp.sum(lq2_ref[...] * lk2_ref[...], axis=-1, keepdims=True))
           + LAM_INIT_0)
    oT = acc_s[0] * (1.0 / l_s[0]) - lam * (acc_s[1] * (1.0 / l_s[1]))
    oT = oT * lax.rsqrt(jnp.mean(oT * oT, axis=0, keepdims=True) + SUBLN_EPS)
    o = oT.T * (sg_ref[...] * (1.0 - LAM_INIT_0))
    o_ref[0] = (o * _silu(z_ref[0])).astype(BF16)


def _attention(qT, k, vT, z, slopes, subln_g, lq1, lk1, lq2, lk2, B, S):
    t = ATT_TILE
    nk = S // t
    assert S % t == 0 and nk <= DA_V_DIM
    H = DA_HEADS
    dv = DA_V_DIM
    k3 = k.reshape(B, S, D_MODEL)
    z3 = z.reshape(B, S, D_MODEL)
    small = lambda b, h, i, *_: (0, 0)
    kern = functools.partial(_attn_kernel, nk=nk, t=t)
    return pl.pallas_call(
        kern,
        grid_spec=pltpu.PrefetchScalarGridSpec(
            num_scalar_prefetch=1,
            grid=(B, H, nk),
            in_specs=[
                pl.BlockSpec((1, nk, dv, t), lambda b, h, i, *_: (b, 0, h, 0)),
                pl.BlockSpec((1, S, dv), lambda b, h, i, *_: (b, 0, h)),
                pl.BlockSpec((1, nk, dv, t), lambda b, h, i, *_: (b, 0, h, 0)),
                pl.BlockSpec((1, t, dv), lambda b, h, i, *_: (b, i, h)),
                pl.BlockSpec((1, dv), small),
                pl.BlockSpec((1, DA_HEAD_DIM), small),
                pl.BlockSpec((1, DA_HEAD_DIM), small),
                pl.BlockSpec((1, DA_HEAD_DIM), small),
                pl.BlockSpec((1, DA_HEAD_DIM), small),
            ],
            out_specs=pl.BlockSpec((1, t, dv), lambda b, h, i, *_: (b, i, h)),
            scratch_shapes=[
                pltpu.SMEM((1,), jnp.int32),
                pltpu.VMEM((2, S, dv), BF16),
                pltpu.VMEM((6, dv, t), BF16),
                pltpu.VMEM((2, 1, t), F32),
                pltpu.VMEM((2, 1, t), F32),
                pltpu.VMEM((2, dv, t), F32),
            ],
        ),
        out_shape=jax.ShapeDtypeStruct((B, S, D_MODEL), BF16),
        compiler_params=_cparams(("arbitrary", "arbitrary", "arbitrary")),
        name="attention",
    )(slopes, qT, k3, vT, z3, subln_g, lq1, lk1, lq2, lk2)


def _mid_kernel(og_ref, x_ref, wo_ref, g_ref, wu_ref, wz_ref, x1_ref, u_ref, z_ref):
    x1 = x_ref[...] + _dot(og_ref[...], wo_ref[...])
    x1_ref[...] = x1
    h = _rms(x1, g_ref[...], EPS).astype(BF16)
    u_ref[...] = _dot(h, wu_ref[...]).astype(BF16)
    z_ref[...] = _dot(h, wz_ref[...])


def _mid(og2d, x2d, wo, g, wu, wz):
    M, D = x2d.shape
    tm = ROW_TILE
    rows = pl.BlockSpec((tm, D), lambda i: (i, 0))
    const = lambda i: (0, 0)
    return pl.pallas_call(
        _mid_kernel,
        grid=(M // tm,),
        in_specs=[rows, rows, pl.BlockSpec((D, D), const), pl.BlockSpec((1, D), const),
                  pl.BlockSpec((D, D), const), pl.BlockSpec((D, D), const)],
        out_specs=[rows, rows, rows],
        out_shape=[jax.ShapeDtypeStruct((M, D), F32),
                   jax.ShapeDtypeStruct((M, D), BF16),
                   jax.ShapeDtypeStruct((M, D), F32)],
        compiler_params=_cparams(("arbitrary",)),
        name="mid",
    )(og2d, x2d, wo, g, wu, wz)


def _dft_factors(S):
    n2 = 1 << (int(math.log2(S)) // 2)
    return S // n2, n2


@functools.lru_cache(maxsize=None)
def _dft_tables(S):
    n1, n2 = _dft_factors(S)
    k1 = np.arange(n1, dtype=np.int64)
    t1 = np.arange(n1, dtype=np.int64)
    t2 = np.arange(n2, dtype=np.int64)
    ph = (k1[None, :, None] * (n2 * t1[None, None, :] + t2[:, None, None])) % S
    ang = 2.0 * np.pi * ph.astype(np.float64) / S
    sc1 = 1.0 / np.sqrt(n1)
    tab1 = np.concatenate([np.cos(ang), -np.sin(ang)], axis=1) * sc1
    a2 = 2.0 * np.pi * ((t2[:, None] * t2[None, :]) % n2).astype(np.float64) / n2
    c2, s2 = np.cos(a2) / np.sqrt(n2), np.sin(a2) / np.sqrt(n2)
    ga = np.concatenate([c2, -s2], axis=0)
    gb = np.concatenate([s2, c2], axis=0)
    c = np.arange(FN_GROUP_DIM, dtype=np.int64)
    a3 = 2.0 * np.pi * ((c[:, None] * c[None, :]) % FN_GROUP_DIM).astype(np.float64) / FN_GROUP_DIM
    sc3 = 1.0 / np.sqrt(FN_GROUP_DIM)
    to = lambda a: np.asarray(a, dtype=np.float32)
    return to(tab1), to(ga), to(gb), to(np.cos(a3) * sc3), to(np.sin(a3) * sc3)


def _dft1_kernel(u_ref, tab_ref, y_ref):
    y_ref[0] = _dot(tab_ref[0], u_ref[0]).astype(BF16)


def _dft1(u3, tab1, B, n1, n2):
    C = D_MODEL
    return pl.pallas_call(
        _dft1_kernel,
        grid=(B, n2),
        in_specs=[pl.BlockSpec((1, n1, C), lambda b, t: (b, 0, t)),
                  pl.BlockSpec((1, 2 * n1, n1), lambda b, t: (t, 0, 0))],
        out_specs=pl.BlockSpec((1, 2 * n1, C), lambda b, t: (b, 0, t)),
        out_shape=jax.ShapeDtypeStruct((B, 2 * n1, n2 * C), BF16),
        compiler_params=_cparams(("arbitrary", "arbitrary")),
        name="dft1",
    )(u3, tab1)


def _dft2_kernel(y_ref, ga_ref, gb_ref, c3_ref, s3_ref, f_ref, *, n2):
    for kk in range(DFT_KB):
        x = _dot(ga_ref[...], y_ref[0, 0, kk]) + _dot(gb_ref[...], y_ref[0, 1, kk])
        xr = x[:n2].astype(BF16)
        xi = x[n2:].astype(BF16)
        for g in range(FN_GROUPS):
            sl = slice(g * FN_GROUP_DIM, (g + 1) * FN_GROUP_DIM)
            f_ref[0, :, kk, sl] = _dot(xr[:, sl], c3_ref[...]) + _dot(xi[:, sl], s3_ref[...])


def _dft2(y5, ga, gb, c3, s3, B, n1, n2):
    C = D_MODEL
    const = lambda b, k: (0, 0)
    return pl.pallas_call(
        functools.partial(_dft2_kernel, n2=n2),
        grid=(B, n1 // DFT_KB),
        in_specs=[pl.BlockSpec((1, 2, DFT_KB, n2, C), lambda b, k: (b, 0, k, 0, 0)),
                  pl.BlockSpec((2 * n2, n2), const), pl.BlockSpec((2 * n2, n2), const),
                  pl.BlockSpec((FN_GROUP_DIM, FN_GROUP_DIM), const),
                  pl.BlockSpec((FN_GROUP_DIM, FN_GROUP_DIM), const)],
        out_specs=pl.BlockSpec((1, n2, DFT_KB, C), lambda b, k: (b, 0, k, 0)),
        out_shape=jax.ShapeDtypeStruct((B, n2, n1, C), F32),
        compiler_params=_cparams(("arbitrary", "arbitrary")),
        name="dft2",
    )(y5, ga, gb, c3, s3)


def _final_kernel(f_ref, z_ref, x1_ref, wo_ref, g_ref, y_ref):
    a = (f_ref[...] * _silu(z_ref[...])).astype(BF16)
    x2 = x1_ref[...] + _dot(a, wo_ref[...])
    y_ref[...] = _rms(x2, g_ref[...], EPS)


def _final(f2d, z2d, x1, wo, g):
    M, D = x1.shape
    tm = ROW_TILE
    rows = pl.BlockSpec((tm, D), lambda i: (i, 0))
    const = lambda i: (0, 0)
    return pl.pallas_call(
        _final_kernel,
        grid=(M // tm,),
        in_specs=[rows, rows, rows, pl.BlockSpec((D, D), const), pl.BlockSpec((1, D), const)],
        out_specs=rows,
        out_shape=jax.ShapeDtypeStruct((M, D), F32),
        compiler_params=_cparams(("arbitrary",)),
        name="final",
    )(f2d, z2d, x1, wo, g)


def _trunk(x, w):
    B, S, D = x.shape
    M = B * S
    x2d = x.reshape(M, D)
    qT, k, vT, z = _inproj(x2d, w["attn_norm"], w["wqT"], w["wk"], w["wvT"], w["wz"], B, S)
    og = _attention(qT, k, vT, z, w["slopes"], w["subln"], w["lq1"], w["lk1"], w["lq2"],
                    w["lk2"], B, S)
    x1, u, z2 = _mid(og.reshape(M, D), x2d, w["attn_wo"], w["fnet_norm"], w["wu"], w["wz2"])
    n1, n2 = _dft_factors(S)
    tab1, ga, gb, c3, s3 = (jnp.asarray(a).astype(BF16) for a in _dft_tables(S))
    y = _dft1(u.reshape(B, n1, n2 * D), tab1, B, n1, n2)
    f = _dft2(y.reshape(B, 2, n1, n2, D), ga, gb, c3, s3, B, n1, n2)
    out = _final(f.reshape(M, D), z2, x1, w["fnet_wo"], w["final_norm"])
    return out.reshape(B, S, D)


def kernel(x_prompt, x_sample, attn_norm, attn_w_in, attn_lambda_q1, attn_lambda_k1,
           attn_lambda_q2, attn_lambda_k2, attn_subln, attn_w_out, fnet_norm, fnet_w_in,
           fnet_w_out, final_norm):
    D = D_MODEL
    w_in = attn_w_in[0]
    w = {
        "attn_norm": attn_norm[0].reshape(1, D),
        "wqT": w_in[:, 0 * D:1 * D].T.astype(BF16),
        "wk": w_in[:, 1 * D:2 * D].astype(BF16),
        "wvT": w_in[:, 2 * D:3 * D].T.astype(BF16),
        "wz": w_in[:, 3 * D:4 * D].astype(BF16),
        "slopes": 2.0 ** (-(8.0 / DA_HEADS) * jnp.arange(1, DA_HEADS + 1, dtype=F32)),
        "subln": attn_subln[0].reshape(1, DA_V_DIM),
        "lq1": attn_lambda_q1[0].reshape(1, DA_HEAD_DIM),
        "lk1": attn_lambda_k1[0].reshape(1, DA_HEAD_DIM),
        "lq2": attn_lambda_q2[0].reshape(1, DA_HEAD_DIM),
        "lk2": attn_lambda_k2[0].reshape(1, DA_HEAD_DIM),
        "attn_wo": attn_w_out[0].astype(BF16),
        "fnet_norm": fnet_norm[0].reshape(1, D),
        "wu": fnet_w_in[0][:, :D].astype(BF16),
        "wz2": fnet_w_in[0][:, D:].astype(BF16),
        "fnet_wo": fnet_w_out[0].astype(BF16),
        "final_norm": final_norm.reshape(1, D),
    }
    return (_trunk(x_prompt, w), _trunk(x_sample, w))
```

```python
import functools
import math

import numpy as np
import jax
import jax.numpy as jnp
from jax import lax
from jax.experimental import pallas as pl
from jax.experimental.pallas import tpu as pltpu

D_MODEL = 1024
DA_HEADS = 8
DA_HEAD_DIM = 64
DA_V_DIM = 128
FN_GROUPS = 8
FN_GROUP_DIM = 128
EPS = 1e-6
SUBLN_EPS = 1e-5
LAM_INIT_0 = 0.8 - 0.6 * math.exp(-0.3 * 0)

ROW_TILE = 512
ATT_TILE = 512
DFT_KB = 8
VMEM_LIMIT = 56 * 1024 * 1024
EXP_FLUSH = 88.0
NORM_SLACK = 1.02

F32 = jnp.float32
BF16 = jnp.bfloat16


def _cparams(sem):
    return pltpu.CompilerParams(dimension_semantics=sem, vmem_limit_bytes=VMEM_LIMIT)


def _rms(x, g, eps):
    return x * lax.rsqrt(jnp.mean(x * x, axis=-1, keepdims=True) + eps) * g


def _silu(z):
    return z * (1.0 / (1.0 + jnp.exp(-z)))


def _dot(a, b):
    return jnp.dot(a, b, preferred_element_type=F32)


def _dot_nt(a, b):
    return lax.dot_general(a, b, (((1,), (1,)), ((), ())), preferred_element_type=F32)


def _inproj_kernel(x_ref, g_ref, wqT_ref, wk_ref, wvT_ref, wz_ref,
                   qT_ref, k_ref, vT_ref, z_ref):
    h = _rms(x_ref[...], g_ref[...], EPS).astype(BF16)
    qT_ref[0, 0] = (_dot_nt(wqT_ref[...], h) * (DA_HEAD_DIM ** -0.5)).astype(BF16)
    vT_ref[0, 0] = _dot_nt(wvT_ref[...], h).astype(BF16)
    k_ref[...] = _dot(h, wk_ref[...]).astype(BF16)
    z_ref[...] = _dot(h, wz_ref[...])


def _inproj(x2d, g, wqT, wk, wvT, wz, B, S):
    M, D = x2d.shape
    tm = ROW_TILE
    nb = S // tm
    const = lambda i: (0, 0)
    return pl.pallas_call(
        _inproj_kernel,
        grid=(M // tm,),
        in_specs=[
            pl.BlockSpec((tm, D), lambda i: (i, 0)),
            pl.BlockSpec((1, D), const),
            pl.BlockSpec((D, D), const),
            pl.BlockSpec((D, D), const),
            pl.BlockSpec((D, D), const),
            pl.BlockSpec((D, D), const),
        ],
        out_specs=[
            pl.BlockSpec((1, 1, D, tm), lambda i: (i // nb, i % nb, 0, 0)),
            pl.BlockSpec((tm, D), lambda i: (i, 0)),
            pl.BlockSpec((1, 1, D, tm), lambda i: (i // nb, i % nb, 0, 0)),
            pl.BlockSpec((tm, D), lambda i: (i, 0)),
        ],
        out_shape=[
            jax.ShapeDtypeStruct((B, nb, D, tm), BF16),
            jax.ShapeDtypeStruct((M, D), BF16),
            jax.ShapeDtypeStruct((B, nb, D, tm), BF16),
            jax.ShapeDtypeStruct((M, D), F32),
        ],
        compiler_params=_cparams(("arbitrary",)),
        name="inproj",
    )(x2d, g, wqT, wk, wvT, wz)


def _attn_kernel(slopes_ref, qT_ref, k_ref, vT_ref, z_ref, sg_ref, lq1_ref, lk1_ref,
                 lq2_ref, lk2_ref, o_ref, w_s, kx_s, qv_s, db_s, m_s, l_s, acc_s, *, nk, t):
    h = pl.program_id(1)
    i = pl.program_id(2)
    slope = slopes_ref[h]

    def split(x):
        hi = ((x >> 4) << 4).astype(F32) * slope
        lo = (x & 15).astype(F32) * slope
        return hi, lo

    @pl.when(i == 0)
    def _():
        lane = lax.broadcasted_iota(jnp.int32, (t, DA_V_DIM), 1)
        lane_lo = lane < DA_HEAD_DIM
        c_hi, c_lo = split(lax.broadcasted_iota(jnp.int32, (t, DA_V_DIM), 0))
        le = lane & (DA_HEAD_DIM - 1)
        k_extra = jnp.where(le == 0, c_hi, jnp.where(le == 1, c_lo,
                                                     jnp.where(le < 4, 1.0, 0.0))).astype(BF16)

        def nbody(j, c):
            k1, k2, q1, q2 = c
            rows = pl.ds(pl.multiple_of(j * t, t), t)
            kb16 = k_ref[0, rows, :]
            kx_s[0, rows, :] = jnp.where(lane_lo, kb16, k_extra)
            kx_s[1, rows, :] = jnp.where(lane_lo, k_extra, kb16)
            kf = kb16.astype(F32)
            ksq = kf * kf
            ka = jnp.sum(jnp.where(lane_lo, ksq, 0.0), axis=1, keepdims=True)
            kb = jnp.sum(jnp.where(lane_lo, 0.0, ksq), axis=1, keepdims=True)
            qf = qT_ref[0, j].astype(F32)
            qsq = qf * qf
            qa = jnp.sum(qsq[:DA_HEAD_DIM], axis=0, keepdims=True)
            qb = jnp.sum(qsq[DA_HEAD_DIM:], axis=0, keepdims=True)
            return (jnp.maximum(k1, jnp.max(ka, axis=0, keepdims=True)),
                    jnp.maximum(k2, jnp.max(kb, axis=0, keepdims=True)),
                    jnp.maximum(q1, jnp.max(qa, axis=1, keepdims=True)),
                    jnp.maximum(q2, jnp.max(qb, axis=1, keepdims=True)))

        zero = jnp.zeros((1, 1), F32)
        k1, k2, q1, q2 = lax.fori_loop(0, nk, nbody, (zero, zero, zero, zero))
        r = jnp.sqrt(jnp.maximum(k1 * q1, k2 * q2)) * NORM_SLACK
        wf = (2.0 * r + EXP_FLUSH) / (slope * t)
        n = lax.broadcasted_iota(jnp.int32, (1, DA_V_DIM), 1) + 1
        hit = jnp.logical_and(n <= nk, n.astype(F32) <= wf)
        cnt = jnp.sum(jnp.where(hit, 1.0, 0.0), axis=1, keepdims=True)
        w_s[0] = cnt.astype(jnp.int32)[0, 0] + 2
        rel = (lax.broadcasted_iota(jnp.int32, (t, t), 0)
               - lax.broadcasted_iota(jnp.int32, (t, t), 1))
        db_s[...] = jnp.abs(rel).astype(F32) * slope

    qT = qT_ref[0, i]
    row = lax.broadcasted_iota(jnp.int32, qT.shape, 0)
    row_lo = row < DA_HEAD_DIM
    r_hi, r_lo = split(lax.broadcasted_iota(jnp.int32, qT.shape, 1))
    re = row & (DA_HEAD_DIM - 1)
    q_extra = jnp.where(re < 2, 1.0, jnp.where(re == 2, -r_hi, jnp.where(re == 3, -r_lo, 0.0)))
    for ver, ext in enumerate((q_extra, jnp.zeros_like(q_extra), -q_extra)):
        e16 = ext.astype(BF16)
        qv_s[2 * ver] = jnp.where(row_lo, qT, e16)
        qv_s[2 * ver + 1] = jnp.where(row_lo, e16, qT)
    m_s[...] = jnp.full(m_s.shape, -jnp.inf, F32)
    l_s[...] = jnp.zeros(l_s.shape, F32)
    acc_s[...] = jnp.zeros(acc_s.shape, F32)

    def step(js, ver, diag_bias=None):
        rows = [pl.ds(pl.multiple_of(j * t, t), t) for j in js]
        cjs = [(jnp.zeros((1, t), jnp.int32) + jnp.abs(i - j) * t).astype(F32) * (-slope)
               for j in js]
        ss = [[_dot(kx_s[mi, r, :], qv_s[2 * ver + mi]) for r in rows] for mi in range(2)]
        pv, al = [], []
        for mi in range(2):
            sm = ss[mi] if diag_bias is None else [s - diag_bias for s in ss[mi]]
            m_old = m_s[mi]
            m_new = m_old
            for s, cj in zip(sm, cjs):
                m_new = jnp.maximum(m_new, jnp.max(s, axis=0, keepdims=True) + cj)
            alpha = jnp.exp(m_old - m_new)
            l_new = alpha * l_s[mi]
            o = None
            for s, cj, j in zip(sm, cjs, js):
                p = jnp.exp(s - (m_new - cj))
                l_new = l_new + jnp.sum(p, axis=0, keepdims=True)
                d = _dot(vT_ref[0, j], p.astype(BF16))
                o = d if o is None else o + d
            l_s[mi] = l_new
            m_s[mi] = m_new
            pv.append(o)
            al.append(alpha)
        for mi in range(2):
            acc_s[mi] = al[mi] * acc_s[mi] + pv[mi]

    step([i], 1, db_s[...])

    w = w_s[0]

    def sweep(lo, hi, ver):
        n = hi - lo

        def quad(q, carry):
            step([lo + 4 * q + d for d in range(4)], ver)
            return carry

        lax.fori_loop(0, n >> 2, quad, 0)
        rest = lo + ((n >> 2) << 2)

        @pl.when((n & 2) == 2)
        def _():
            step([rest, rest + 1], ver)

        @pl.when((n & 1) == 1)
        def _():
            step([hi - 1], ver)

    sweep(jnp.maximum(i - w + 1, 0), i, 0)
    sweep(i + 1, jnp.minimum(i + w, nk), 2)

    lam = (jnp.exp(jnp.sum(lq1_ref[...] * lk1_ref[...], axis=-1, keepdims=True))
           - jnp.exp(jnp.sum(lq2_ref[...] * lk2_ref[...], axis=-1, keepdims=True))
           + LAM_INIT_0)
    oT = acc_s[0] * (1.0 / l_s[0]) - lam * (acc_s[1] * (1.0 / l_s[1]))
    oT = oT * lax.rsqrt(jnp.mean(oT * oT, axis=0, keepdims=True) + SUBLN_EPS)
    o = oT.T * (sg_ref[...] * (1.0 - LAM_INIT_0))
    o_ref[0] = (o * _silu(z_ref[0])).astype(BF16)


def _attention(qT, k, vT, z, slopes, subln_g, lq1, lk1, lq2, lk2, B, S):
    t = ATT_TILE
    nk = S // t
    assert S % t == 0 and nk <= DA_V_DIM
    H = DA_HEADS
    dv = DA_V_DIM
    k3 = k.reshape(B, S, D_MODEL)
    z3 = z.reshape(B, S, D_MODEL)
    small = lambda b, h, i, *_: (0, 0)
    kern = functools.partial(_attn_kernel, nk=nk, t=t)
    return pl.pallas_call(
        kern,
        grid_spec=pltpu.PrefetchScalarGridSpec(
            num_scalar_prefetch=1,
            grid=(B, H, nk),
            in_specs=[
                pl.BlockSpec((1, nk, dv, t), lambda b, h, i, *_: (b, 0, h, 0)),
                pl.BlockSpec((1, S, dv), lambda b, h, i, *_: (b, 0, h)),
                pl.BlockSpec((1, nk, dv, t), lambda b, h, i, *_: (b, 0, h, 0)),
                pl.BlockSpec((1, t, dv), lambda b, h, i, *_: (b, i, h)),
                pl.BlockSpec((1, dv), small),
                pl.BlockSpec((1, DA_HEAD_DIM), small),
                pl.BlockSpec((1, DA_HEAD_DIM), small),
                pl.BlockSpec((1, DA_HEAD_DIM), small),
                pl.BlockSpec((1, DA_HEAD_DIM), small),
            ],
            out_specs=pl.BlockSpec((1, t, dv), lambda b, h, i, *_: (b, i, h)),
            scratch_shapes=[
                pltpu.SMEM((1,), jnp.int32),
                pltpu.VMEM((2, S, dv), BF16),
                pltpu.VMEM((6, dv, t), BF16),
                pltpu.VMEM((t, t), F32),
                pltpu.VMEM((2, 1, t), F32),
                pltpu.VMEM((2, 1, t), F32),
                pltpu.VMEM((2, dv, t), F32),
            ],
        ),
        out_shape=jax.ShapeDtypeStruct((B, S, D_MODEL), BF16),
        compiler_params=_cparams(("arbitrary", "arbitrary", "arbitrary")),
        name="attention",
    )(slopes, qT, k3, vT, z3, subln_g, lq1, lk1, lq2, lk2)


def _mid_kernel(og_ref, x_ref, wo_ref, g_ref, wu_ref, wz_ref, x1_ref, u_ref, z_ref):
    x1 = x_ref[...] + _dot(og_ref[...], wo_ref[...])
    x1_ref[...] = x1
    h = _rms(x1, g_ref[...], EPS).astype(BF16)
    u_ref[...] = _dot(h, wu_ref[...]).astype(BF16)
    z_ref[...] = _dot(h, wz_ref[...])


def _mid(og2d, x2d, wo, g, wu, wz):
    M, D = x2d.shape
    tm = ROW_TILE
    rows = pl.BlockSpec((tm, D), lambda i: (i, 0))
    const = lambda i: (0, 0)
    return pl.pallas_call(
        _mid_kernel,
        grid=(M // tm,),
        in_specs=[rows, rows, pl.BlockSpec((D, D), const), pl.BlockSpec((1, D), const),
                  pl.BlockSpec((D, D), const), pl.BlockSpec((D, D), const)],
        out_specs=[rows, rows, rows],
        out_shape=[jax.ShapeDtypeStruct((M, D), F32),
                   jax.ShapeDtypeStruct((M, D), BF16),
                   jax.ShapeDtypeStruct((M, D), F32)],
        compiler_params=_cparams(("arbitrary",)),
        name="mid",
    )(og2d, x2d, wo, g, wu, wz)


def _dft_factors(S):
    n2 = 1 << (int(math.log2(S)) // 2)
    return S // n2, n2


@functools.lru_cache(maxsize=None)
def _dft_tables(S):
    n1, n2 = _dft_factors(S)
    k1 = np.arange(n1, dtype=np.int64)
    t1 = np.arange(n1, dtype=np.int64)
    t2 = np.arange(n2, dtype=np.int64)
    ph = (k1[None, :, None] * (n2 * t1[None, None, :] + t2[:, None, None])) % S
    ang = 2.0 * np.pi * ph.astype(np.float64) / S
    sc1 = 1.0 / np.sqrt(n1)
    tab1 = np.concatenate([np.cos(ang), -np.sin(ang)], axis=1) * sc1
    a2 = 2.0 * np.pi * ((t2[:, None] * t2[None, :]) % n2).astype(np.float64) / n2
    c2, s2 = np.cos(a2) / np.sqrt(n2), np.sin(a2) / np.sqrt(n2)
    ga = np.concatenate([c2, -s2], axis=0)
    gb = np.concatenate([s2, c2], axis=0)
    c = np.arange(FN_GROUP_DIM, dtype=np.int64)
    a3 = 2.0 * np.pi * ((c[:, None] * c[None, :]) % FN_GROUP_DIM).astype(np.float64) / FN_GROUP_DIM
    sc3 = 1.0 / np.sqrt(FN_GROUP_DIM)
    to = lambda a: np.asarray(a, dtype=np.float32)
    return to(tab1), to(ga), to(gb), to(np.cos(a3) * sc3), to(np.sin(a3) * sc3)


def _dft1_kernel(u_ref, tab_ref, y_ref):
    y_ref[0] = _dot(tab_ref[0], u_ref[0]).astype(BF16)


def _dft1(u3, tab1, B, n1, n2):
    C = D_MODEL
    return pl.pallas_call(
        _dft1_kernel,
        grid=(B, n2),
        in_specs=[pl.BlockSpec((1, n1, C), lambda b, t: (b, 0, t)),
                  pl.BlockSpec((1, 2 * n1, n1), lambda b, t: (t, 0, 0))],
        out_specs=pl.BlockSpec((1, 2 * n1, C), lambda b, t: (b, 0, t)),
        out_shape=jax.ShapeDtypeStruct((B, 2 * n1, n2 * C), BF16),
        compiler_params=_cparams(("arbitrary", "arbitrary")),
        name="dft1",
    )(u3, tab1)


def _dft2_kernel(y_ref, ga_ref, gb_ref, c3_ref, s3_ref, f_ref, xr_s, xi_s, *, n2):
    for kk in range(DFT_KB):
        x = _dot(ga_ref[...], y_ref[0, 0, kk]) + _dot(gb_ref[...], y_ref[0, 1, kk])
        xr_s[kk * n2:(kk + 1) * n2, :] = x[:n2].astype(BF16)
        xi_s[kk * n2:(kk + 1) * n2, :] = x[n2:].astype(BF16)
    for g in range(FN_GROUPS):
        sl = slice(g * FN_GROUP_DIM, (g + 1) * FN_GROUP_DIM)
        fg = _dot(xr_s[:, sl], c3_ref[...]) + _dot(xi_s[:, sl], s3_ref[...])
        for kk in range(DFT_KB):
            f_ref[0, :, kk, sl] = fg[kk * n2:(kk + 1) * n2]


def _dft2(y5, ga, gb, c3, s3, B, n1, n2):
    C = D_MODEL
    const = lambda b, k: (0, 0)
    return pl.pallas_call(
        functools.partial(_dft2_kernel, n2=n2),
        grid=(B, n1 // DFT_KB),
        in_specs=[pl.BlockSpec((1, 2, DFT_KB, n2, C), lambda b, k: (b, 0, k, 0, 0)),
                  pl.BlockSpec((2 * n2, n2), const), pl.BlockSpec((2 * n2, n2), const),
                  pl.BlockSpec((FN_GROUP_DIM, FN_GROUP_DIM), const),
                  pl.BlockSpec((FN_GROUP_DIM, FN_GROUP_DIM), const)],
        out_specs=pl.BlockSpec((1, n2, DFT_KB, C), lambda b, k: (b, 0, k, 0)),
        out_shape=jax.ShapeDtypeStruct((B, n2, n1, C), F32),
        scratch_shapes=[pltpu.VMEM((DFT_KB * n2, C), BF16), pltpu.VMEM((DFT_KB * n2, C), BF16)],
        compiler_params=_cparams(("arbitrary", "arbitrary")),
        name="dft2",
    )(y5, ga, gb, c3, s3)


def _final_kernel(f_ref, z_ref, x1_ref, wo_ref, g_ref, y_ref):
    a = (f_ref[...] * _silu(z_ref[...])).astype(BF16)
    x2 = x1_ref[...] + _dot(a, wo_ref[...])
    y_ref[...] = _rms(x2, g_ref[...], EPS)


def _final(f2d, z2d, x1, wo, g):
    M, D = x1.shape
    tm = ROW_TILE
    rows = pl.BlockSpec((tm, D), lambda i: (i, 0))
    const = lambda i: (0, 0)
    return pl.pallas_call(
        _final_kernel,
        grid=(M // tm,),
        in_specs=[rows, rows, rows, pl.BlockSpec((D, D), const), pl.BlockSpec((1, D), const)],
        out_specs=rows,
        out_shape=jax.ShapeDtypeStruct((M, D), F32),
        compiler_params=_cparams(("arbitrary",)),
        name="final",
    )(f2d, z2d, x1, wo, g)


def _trunk(x, w):
    B, S, D = x.shape
    M = B * S
    x2d = x.reshape(M, D)
    qT, k, vT, z = _inproj(x2d, w["attn_norm"], w["wqT"], w["wk"], w["wvT"], w["wz"], B, S)
    og = _attention(qT, k, vT, z, w["slopes"], w["subln"], w["lq1"], w["lk1"], w["lq2"],
                    w["lk2"], B, S)
    x1, u, z2 = _mid(og.reshape(M, D), x2d, w["attn_wo"], w["fnet_norm"], w["wu"], w["wz2"])
    n1, n2 = _dft_factors(S)
    tab1, ga, gb, c3, s3 = (jnp.asarray(a).astype(BF16) for a in _dft_tables(S))
    y = _dft1(u.reshape(B, n1, n2 * D), tab1, B, n1, n2)
    f = _dft2(y.reshape(B, 2, n1, n2, D), ga, gb, c3, s3, B, n1, n2)
    out = _final(f.reshape(M, D), z2, x1, w["fnet_wo"], w["final_norm"])
    return out.reshape(B, S, D)


def kernel(x_prompt, x_sample, attn_norm, attn_w_in, attn_lambda_q1, attn_lambda_k1,
           attn_lambda_q2, attn_lambda_k2, attn_subln, attn_w_out, fnet_norm, fnet_w_in,
           fnet_w_out, final_norm):
    D = D_MODEL
    w_in = attn_w_in[0]
    w = {
        "attn_norm": attn_norm[0].reshape(1, D),
        "wqT": w_in[:, 0 * D:1 * D].T.astype(BF16),
        "wk": w_in[:, 1 * D:2 * D].astype(BF16),
        "wvT": w_in[:, 2 * D:3 * D].T.astype(BF16),
        "wz": w_in[:, 3 * D:4 * D].astype(BF16),
        "slopes": 2.0 ** (-(8.0 / DA_HEADS) * jnp.arange(1, DA_HEADS + 1, dtype=F32)),
        "subln": attn_subln[0].reshape(1, DA_V_DIM),
        "lq1": attn_lambda_q1[0].reshape(1, DA_HEAD_DIM),
        "lk1": attn_lambda_k1[0].reshape(1, DA_HEAD_DIM),
        "lq2": attn_lambda_q2[0].reshape(1, DA_HEAD_DIM),
        "lk2": attn_lambda_k2[0].reshape(1, DA_HEAD_DIM),
        "attn_wo": attn_w_out[0].astype(BF16),
        "fnet_norm": fnet_norm[0].reshape(1, D),
        "wu": fnet_w_in[0][:, :D].astype(BF16),
        "wz2": fnet_w_in[0][:, D:].astype(BF16),
        "fnet_wo": fnet_w_out[0].astype(BF16),
        "final_norm": final_norm.reshape(1, D),
    }
    return (_trunk(x_prompt, w), _trunk(x_sample, w))
```

```python
import functools
import math

import numpy as np
import jax
import jax.numpy as jnp
from jax import lax
from jax.experimental import pallas as pl
from jax.experimental.pallas import tpu as pltpu

D_MODEL = 1024
DA_HEADS = 8
DA_HEAD_DIM = 64
DA_V_DIM = 128
FN_GROUPS = 8
FN_GROUP_DIM = 128
EPS = 1e-6
SUBLN_EPS = 1e-5
LAM_INIT_0 = 0.8 - 0.6 * math.exp(-0.3 * 0)

ROW_TILE = 512
ATT_TILE = 512
DFT_KB = 8
DFT_TB = 4
VMEM_LIMIT = 56 * 1024 * 1024
EXP_FLUSH = 88.0
NORM_SLACK = 1.02

F32 = jnp.float32
BF16 = jnp.bfloat16


def _cparams(sem):
    return pltpu.CompilerParams(dimension_semantics=sem, vmem_limit_bytes=VMEM_LIMIT)


def _rms(x, g, eps):
    return x * lax.rsqrt(jnp.mean(x * x, axis=-1, keepdims=True) + eps) * g


def _silu(z):
    return z * (1.0 / (1.0 + jnp.exp(-z)))


def _dot(a, b):
    return jnp.dot(a, b, preferred_element_type=F32)


def _dot_nt(a, b):
    return lax.dot_general(a, b, (((1,), (1,)), ((), ())), preferred_element_type=F32)


def _inproj_kernel(x_ref, g_ref, wqT_ref, wk_ref, wvT_ref, wz_ref,
                   qT_ref, k_ref, vT_ref, z_ref):
    h = _rms(x_ref[...], g_ref[...], EPS).astype(BF16)
    qT_ref[0, 0] = (_dot_nt(wqT_ref[...], h) * (DA_HEAD_DIM ** -0.5)).astype(BF16)
    vT_ref[0, 0] = _dot_nt(wvT_ref[...], h).astype(BF16)
    k_ref[...] = _dot(h, wk_ref[...]).astype(BF16)
    z_ref[...] = _dot(h, wz_ref[...])


def _inproj(x2d, g, wqT, wk, wvT, wz, B, S):
    M, D = x2d.shape
    tm = ROW_TILE
    nb = S // tm
    const = lambda i: (0, 0)
    return pl.pallas_call(
        _inproj_kernel,
        grid=(M // tm,),
        in_specs=[
            pl.BlockSpec((tm, D), lambda i: (i, 0)),
            pl.BlockSpec((1, D), const),
            pl.BlockSpec((D, D), const),
            pl.BlockSpec((D, D), const),
            pl.BlockSpec((D, D), const),
            pl.BlockSpec((D, D), const),
        ],
        out_specs=[
            pl.BlockSpec((1, 1, D, tm), lambda i: (i // nb, i % nb, 0, 0)),
            pl.BlockSpec((tm, D), lambda i: (i, 0)),
            pl.BlockSpec((1, 1, D, tm), lambda i: (i // nb, i % nb, 0, 0)),
            pl.BlockSpec((tm, D), lambda i: (i, 0)),
        ],
        out_shape=[
            jax.ShapeDtypeStruct((B, nb, D, tm), BF16),
            jax.ShapeDtypeStruct((M, D), BF16),
            jax.ShapeDtypeStruct((B, nb, D, tm), BF16),
            jax.ShapeDtypeStruct((M, D), F32),
        ],
        compiler_params=_cparams(("arbitrary",)),
        name="inproj",
    )(x2d, g, wqT, wk, wvT, wz)


def _attn_kernel(slopes_ref, qT_ref, k_ref, vT_ref, z_ref, sg_ref, lq1_ref, lk1_ref,
                 lq2_ref, lk2_ref, o_ref, w_s, kx_s, qv_s, db_s, m_s, l_s, acc_s, *, nk, t):
    h = pl.program_id(1)
    i = pl.program_id(2)
    slope = slopes_ref[h]

    def split(x):
        hi = ((x >> 4) << 4).astype(F32) * slope
        lo = (x & 15).astype(F32) * slope
        return hi, lo

    @pl.when(i == 0)
    def _():
        lane = lax.broadcasted_iota(jnp.int32, (t, DA_V_DIM), 1)
        lane_lo = lane < DA_HEAD_DIM
        c_hi, c_lo = split(lax.broadcasted_iota(jnp.int32, (t, DA_V_DIM), 0))
        le = lane & (DA_HEAD_DIM - 1)
        k_extra = jnp.where(le == 0, c_hi, jnp.where(le == 1, c_lo,
                                                     jnp.where(le < 4, 1.0, 0.0))).astype(BF16)

        def nbody(j, c):
            k1, k2, q1, q2 = c
            rows = pl.ds(pl.multiple_of(j * t, t), t)
            kb16 = k_ref[0, rows, :]
            kx_s[0, rows, :] = jnp.where(lane_lo, kb16, k_extra)
            kx_s[1, rows, :] = jnp.where(lane_lo, k_extra, kb16)
            kf = kb16.astype(F32)
            ksq = kf * kf
            ka = jnp.sum(jnp.where(lane_lo, ksq, 0.0), axis=1, keepdims=True)
            kb = jnp.sum(jnp.where(lane_lo, 0.0, ksq), axis=1, keepdims=True)
            qf = qT_ref[0, j].astype(F32)
            qsq = qf * qf
            qa = jnp.sum(qsq[:DA_HEAD_DIM], axis=0, keepdims=True)
            qb = jnp.sum(qsq[DA_HEAD_DIM:], axis=0, keepdims=True)
            return (jnp.maximum(k1, jnp.max(ka, axis=0, keepdims=True)),
                    jnp.maximum(k2, jnp.max(kb, axis=0, keepdims=True)),
                    jnp.maximum(q1, jnp.max(qa, axis=1, keepdims=True)),
                    jnp.maximum(q2, jnp.max(qb, axis=1, keepdims=True)))

        zero = jnp.zeros((1, 1), F32)
        k1, k2, q1, q2 = lax.fori_loop(0, nk, nbody, (zero, zero, zero, zero))
        r = jnp.sqrt(jnp.maximum(k1 * q1, k2 * q2)) * NORM_SLACK
        wf = (2.0 * r + EXP_FLUSH) / (slope * t)
        n = lax.broadcasted_iota(jnp.int32, (1, DA_V_DIM), 1) + 1
        hit = jnp.logical_and(n <= nk, n.astype(F32) <= wf)
        cnt = jnp.sum(jnp.where(hit, 1.0, 0.0), axis=1, keepdims=True)
        w_s[0] = cnt.astype(jnp.int32)[0, 0] + 2
        rel = (lax.broadcasted_iota(jnp.int32, (t, t), 0)
               - lax.broadcasted_iota(jnp.int32, (t, t), 1))
        db_s[...] = jnp.abs(rel).astype(F32) * slope

    qT = qT_ref[0, i]
    row = lax.broadcasted_iota(jnp.int32, qT.shape, 0)
    row_lo = row < DA_HEAD_DIM
    r_hi, r_lo = split(lax.broadcasted_iota(jnp.int32, qT.shape, 1))
    re = row & (DA_HEAD_DIM - 1)
    q_extra = jnp.where(re < 2, 1.0, jnp.where(re == 2, -r_hi, jnp.where(re == 3, -r_lo, 0.0)))
    for ver, ext in enumerate((q_extra, jnp.zeros_like(q_extra), -q_extra)):
        e16 = ext.astype(BF16)
        qv_s[2 * ver] = jnp.where(row_lo, qT, e16)
        qv_s[2 * ver + 1] = jnp.where(row_lo, e16, qT)
    m_s[...] = jnp.full(m_s.shape, -jnp.inf, F32)
    l_s[...] = jnp.zeros(l_s.shape, F32)
    acc_s[...] = jnp.zeros(acc_s.shape, F32)

    def step(js, ver, diag_bias=None):
        rows = [pl.ds(pl.multiple_of(j * t, t), t) for j in js]
        cjs = [(jnp.zeros((1, t), jnp.int32) + jnp.abs(i - j) * t).astype(F32) * (-slope)
               for j in js]
        ss = [[_dot(kx_s[mi, r, :], qv_s[2 * ver + mi]) for r in rows] for mi in range(2)]
        pv, al = [], []
        for mi in range(2):
            sm = ss[mi] if diag_bias is None else [s - diag_bias for s in ss[mi]]
            m_old = m_s[mi]
            m_new = m_old
            for s, cj in zip(sm, cjs):
                m_new = jnp.maximum(m_new, jnp.max(s, axis=0, keepdims=True) + cj)
            alpha = jnp.exp(m_old - m_new)
            l_new = alpha * l_s[mi]
            o = None
            for s, cj, j in zip(sm, cjs, js):
                p = jnp.exp(s - (m_new - cj))
                l_new = l_new + jnp.sum(p, axis=0, keepdims=True)
                d = _dot(vT_ref[0, j], p.astype(BF16))
                o = d if o is None else o + d
            l_s[mi] = l_new
            m_s[mi] = m_new
            pv.append(o)
            al.append(alpha)
        for mi in range(2):
            acc_s[mi] = al[mi] * acc_s[mi] + pv[mi]

    step([i], 1, db_s[...])

    w = w_s[0]

    def sweep(lo, hi, ver):
        n = hi - lo

        def quad(q, carry):
            step([lo + 4 * q + d for d in range(4)], ver)
            return carry

        lax.fori_loop(0, n >> 2, quad, 0)
        rest = lo + ((n >> 2) << 2)

        @pl.when((n & 2) == 2)
        def _():
            step([rest, rest + 1], ver)

        @pl.when((n & 1) == 1)
        def _():
            step([hi - 1], ver)

    sweep(jnp.maximum(i - w + 1, 0), i, 0)
    sweep(i + 1, jnp.minimum(i + w, nk), 2)

    lam = (jnp.exp(jnp.sum(lq1_ref[...] * lk1_ref[...], axis=-1, keepdims=True))
           - jnp.exp(jnp.sum(lq2_ref[...] * lk2_ref[...], axis=-1, keepdims=True))
           + LAM_INIT_0)
    oT = acc_s[0] * (1.0 / l_s[0]) - lam * (acc_s[1] * (1.0 / l_s[1]))
    oT = oT * lax.rsqrt(jnp.mean(oT * oT, axis=0, keepdims=True) + SUBLN_EPS)
    o = oT.T * (sg_ref[...] * (1.0 - LAM_INIT_0))
    o_ref[0] = (o * _silu(z_ref[0])).astype(BF16)


def _attention(qT, k, vT, z, slopes, subln_g, lq1, lk1, lq2, lk2, B, S):
    t = ATT_TILE
    nk = S // t
    assert S % t == 0 and nk <= DA_V_DIM
    H = DA_HEADS
    dv = DA_V_DIM
    k3 = k.reshape(B, S, D_MODEL)
    z3 = z.reshape(B, S, D_MODEL)
    small = lambda b, h, i, *_: (0, 0)
    kern = functools.partial(_attn_kernel, nk=nk, t=t)
    return pl.pallas_call(
        kern,
        grid_spec=pltpu.PrefetchScalarGridSpec(
            num_scalar_prefetch=1,
            grid=(B, H, nk),
            in_specs=[
                pl.BlockSpec((1, nk, dv, t), lambda b, h, i, *_: (b, 0, h, 0)),
                pl.BlockSpec((1, S, dv), lambda b, h, i, *_: (b, 0, h)),
                pl.BlockSpec((1, nk, dv, t), lambda b, h, i, *_: (b, 0, h, 0)),
                pl.BlockSpec((1, t, dv), lambda b, h, i, *_: (b, i, h)),
                pl.BlockSpec((1, dv), small),
                pl.BlockSpec((1, DA_HEAD_DIM), small),
                pl.BlockSpec((1, DA_HEAD_DIM), small),
                pl.BlockSpec((1, DA_HEAD_DIM), small),
                pl.BlockSpec((1, DA_HEAD_DIM), small),
            ],
            out_specs=pl.BlockSpec((1, t, dv), lambda b, h, i, *_: (b, i, h)),
            scratch_shapes=[
                pltpu.SMEM((1,), jnp.int32),
                pltpu.VMEM((2, S, dv), BF16),
                pltpu.VMEM((6, dv, t), BF16),
                pltpu.VMEM((t, t), F32),
                pltpu.VMEM((2, 1, t), F32),
                pltpu.VMEM((2, 1, t), F32),
                pltpu.VMEM((2, dv, t), F32),
            ],
        ),
        out_shape=jax.ShapeDtypeStruct((B, S, D_MODEL), BF16),
        compiler_params=_cparams(("arbitrary", "arbitrary", "arbitrary")),
        name="attention",
    )(slopes, qT, k3, vT, z3, subln_g, lq1, lk1, lq2, lk2)


def _mid_kernel(og_ref, x_ref, wo_ref, g_ref, wu_ref, wz_ref, x1_ref, u_ref, z_ref):
    x1 = x_ref[...] + _dot(og_ref[...], wo_ref[...])
    x1_ref[...] = x1
    h = _rms(x1, g_ref[...], EPS).astype(BF16)
    u_ref[...] = _dot(h, wu_ref[...]).astype(BF16)
    z_ref[...] = _dot(h, wz_ref[...])


def _mid(og2d, x2d, wo, g, wu, wz):
    M, D = x2d.shape
    tm = ROW_TILE
    rows = pl.BlockSpec((tm, D), lambda i: (i, 0))
    const = lambda i: (0, 0)
    return pl.pallas_call(
        _mid_kernel,
        grid=(M // tm,),
        in_specs=[rows, rows, pl.BlockSpec((D, D), const), pl.BlockSpec((1, D), const),
                  pl.BlockSpec((D, D), const), pl.BlockSpec((D, D), const)],
        out_specs=[rows, rows, rows],
        out_shape=[jax.ShapeDtypeStruct((M, D), F32),
                   jax.ShapeDtypeStruct((M, D), BF16),
                   jax.ShapeDtypeStruct((M, D), F32)],
        compiler_params=_cparams(("arbitrary",)),
        name="mid",
    )(og2d, x2d, wo, g, wu, wz)


def _dft_factors(S):
    n2 = 1 << (int(math.log2(S)) // 2)
    return S // n2, n2


@functools.lru_cache(maxsize=None)
def _dft_tables(S):
    n1, n2 = _dft_factors(S)
    k1 = np.arange(n1, dtype=np.int64)
    t1 = np.arange(n1, dtype=np.int64)
    t2 = np.arange(n2, dtype=np.int64)
    ph = (k1[None, :, None] * (n2 * t1[None, None, :] + t2[:, None, None])) % S
    ang = 2.0 * np.pi * ph.astype(np.float64) / S
    sc1 = 1.0 / np.sqrt(n1)
    tab1 = np.concatenate([np.cos(ang), -np.sin(ang)], axis=1) * sc1
    a2 = 2.0 * np.pi * ((t2[:, None] * t2[None, :]) % n2).astype(np.float64) / n2
    c2, s2 = np.cos(a2) / np.sqrt(n2), np.sin(a2) / np.sqrt(n2)
    ga = np.concatenate([c2, -s2], axis=0)
    gb = np.concatenate([s2, c2], axis=0)
    c = np.arange(FN_GROUP_DIM, dtype=np.int64)
    a3 = 2.0 * np.pi * ((c[:, None] * c[None, :]) % FN_GROUP_DIM).astype(np.float64) / FN_GROUP_DIM
    sc3 = 1.0 / np.sqrt(FN_GROUP_DIM)
    to = lambda a: np.asarray(a, dtype=np.float32)
    return to(tab1), to(ga), to(gb), to(np.cos(a3) * sc3), to(np.sin(a3) * sc3)


def _dft1_kernel(u_ref, tab_ref, y_ref, *, tb):
    C = D_MODEL
    for tt in range(tb):
        sl = slice(tt * C, (tt + 1) * C)
        y_ref[0, :, sl] = _dot(tab_ref[tt], u_ref[0, :, sl]).astype(BF16)


def _dft1(u3, tab1, B, n1, n2):
    C = D_MODEL
    tb = math.gcd(n2, DFT_TB)
    return pl.pallas_call(
        functools.partial(_dft1_kernel, tb=tb),
        grid=(B, n2 // tb),
        in_specs=[pl.BlockSpec((1, n1, tb * C), lambda b, t: (b, 0, t)),
                  pl.BlockSpec((tb, 2 * n1, n1), lambda b, t: (t, 0, 0))],
        out_specs=pl.BlockSpec((1, 2 * n1, tb * C), lambda b, t: (b, 0, t)),
        out_shape=jax.ShapeDtypeStruct((B, 2 * n1, n2 * C), BF16),
        compiler_params=_cparams(("arbitrary", "arbitrary")),
        name="dft1",
    )(u3, tab1)


def _dft2_kernel(y_ref, ga_ref, gb_ref, c3_ref, s3_ref, f_ref, xr_s, xi_s, *, n2):
    for kk in range(DFT_KB):
        x = _dot(ga_ref[...], y_ref[0, 0, kk]) + _dot(gb_ref[...], y_ref[0, 1, kk])
        xr_s[kk * n2:(kk + 1) * n2, :] = x[:n2].astype(BF16)
        xi_s[kk * n2:(kk + 1) * n2, :] = x[n2:].astype(BF16)
    for g in range(FN_GROUPS):
        sl = slice(g * FN_GROUP_DIM, (g + 1) * FN_GROUP_DIM)
        fg = _dot(xr_s[:, sl], c3_ref[...]) + _dot(xi_s[:, sl], s3_ref[...])
        for kk in range(DFT_KB):
            f_ref[0, :, kk, sl] = fg[kk * n2:(kk + 1) * n2]


def _dft2(y5, ga, gb, c3, s3, B, n1, n2):
    C = D_MODEL
    const = lambda b, k: (0, 0)
    return pl.pallas_call(
        functools.partial(_dft2_kernel, n2=n2),
        grid=(B, n1 // DFT_KB),
        in_specs=[pl.BlockSpec((1, 2, DFT_KB, n2, C), lambda b, k: (b, 0, k, 0, 0)),
                  pl.BlockSpec((2 * n2, n2), const), pl.BlockSpec((2 * n2, n2), const),
                  pl.BlockSpec((FN_GROUP_DIM, FN_GROUP_DIM), const),
                  pl.BlockSpec((FN_GROUP_DIM, FN_GROUP_DIM), const)],
        out_specs=pl.BlockSpec((1, n2, DFT_KB, C), lambda b, k: (b, 0, k, 0)),
        out_shape=jax.ShapeDtypeStruct((B, n2, n1, C), F32),
        scratch_shapes=[pltpu.VMEM((DFT_KB * n2, C), BF16), pltpu.VMEM((DFT_KB * n2, C), BF16)],
        compiler_params=_cparams(("arbitrary", "arbitrary")),
        name="dft2",
    )(y5, ga, gb, c3, s3)


def _final_kernel(f_ref, z_ref, x1_ref, wo_ref, g_ref, y_ref):
    a = (f_ref[...] * _silu(z_ref[...])).astype(BF16)
    x2 = x1_ref[...] + _dot(a, wo_ref[...])
    y_ref[...] = _rms(x2, g_ref[...], EPS)


def _final(f2d, z2d, x1, wo, g):
    M, D = x1.shape
    tm = ROW_TILE
    rows = pl.BlockSpec((tm, D), lambda i: (i, 0))
    const = lambda i: (0, 0)
    return pl.pallas_call(
        _final_kernel,
        grid=(M // tm,),
        in_specs=[rows, rows, rows, pl.BlockSpec((D, D), const), pl.BlockSpec((1, D), const)],
        out_specs=rows,
        out_shape=jax.ShapeDtypeStruct((M, D), F32),
        compiler_params=_cparams(("arbitrary",)),
        name="final",
    )(f2d, z2d, x1, wo, g)


def _trunk(x, w):
    B, S, D = x.shape
    M = B * S
    x2d = x.reshape(M, D)
    qT, k, vT, z = _inproj(x2d, w["attn_norm"], w["wqT"], w["wk"], w["wvT"], w["wz"], B, S)
    og = _attention(qT, k, vT, z, w["slopes"], w["subln"], w["lq1"], w["lk1"], w["lq2"],
                    w["lk2"], B, S)
    x1, u, z2 = _mid(og.reshape(M, D), x2d, w["attn_wo"], w["fnet_norm"], w["wu"], w["wz2"])
    n1, n2 = _dft_factors(S)
    tab1, ga, gb, c3, s3 = (jnp.asarray(a).astype(BF16) for a in _dft_tables(S))
    y = _dft1(u.reshape(B, n1, n2 * D), tab1, B, n1, n2)
    f = _dft2(y.reshape(B, 2, n1, n2, D), ga, gb, c3, s3, B, n1, n2)
    out = _final(f.reshape(M, D), z2, x1, w["fnet_wo"], w["final_norm"])
    return out.reshape(B, S, D)


def kernel(x_prompt, x_sample, attn_norm, attn_w_in, attn_lambda_q1, attn_lambda_k1,
           attn_lambda_q2, attn_lambda_k2, attn_subln, attn_w_out, fnet_norm, fnet_w_in,
           fnet_w_out, final_norm):
    D = D_MODEL
    w_in = attn_w_in[0]
    w = {
        "attn_norm": attn_norm[0].reshape(1, D),
        "wqT": w_in[:, 0 * D:1 * D].T.astype(BF16),
        "wk": w_in[:, 1 * D:2 * D].astype(BF16),
        "wvT": w_in[:, 2 * D:3 * D].T.astype(BF16),
        "wz": w_in[:, 3 * D:4 * D].astype(BF16),
        "slopes": 2.0 ** (-(8.0 / DA_HEADS) * jnp.arange(1, DA_HEADS + 1, dtype=F32)),
        "subln": attn_subln[0].reshape(1, DA_V_DIM),
        "lq1": attn_lambda_q1[0].reshape(1, DA_HEAD_DIM),
        "lk1": attn_lambda_k1[0].reshape(1, DA_HEAD_DIM),
        "lq2": attn_lambda_q2[0].reshape(1, DA_HEAD_DIM),
        "lk2": attn_lambda_k2[0].reshape(1, DA_HEAD_DIM),
        "attn_wo": attn_w_out[0].astype(BF16),
        "fnet_norm": fnet_norm[0].reshape(1, D),
        "wu": fnet_w_in[0][:, :D].astype(BF16),
        "wz2": fnet_w_in[0][:, D:].astype(BF16),
        "fnet_wo": fnet_w_out[0].astype(BF16),
        "final_norm": final_norm.reshape(1, D),
    }
    return (_trunk(x_prompt, w), _trunk(x_sample, w))
```
